```python
import math
import jax, jax.numpy as jnp
from jax import lax
import numpy as np

D_MODEL = 1024
BATCH = 4
SEQ = 4096
DEPTH = 4
DEC_BATCH = 128
DEC_SEQ = 8
PAST_LEN = 2048
PAGE_SIZE = 128

N_MIXERS = 3
N_A = len(range(0, DEPTH, N_MIXERS))
N_B = len(range(1, DEPTH, N_MIXERS))
N_C = len(range(2, DEPTH, N_MIXERS))

H_A = 8
D_HA = 64
DV_A = 2 * D_HA
H_C = 16
D_HC = 64
SSM_CH = 16
SSM_G = D_MODEL // SSM_CH
SSM_P = 64
DT_MIN = 1e-3
DT_MAX = 1e-1
N_EXPERTS = 32
TOP_K = 4
D_FF = D_MODEL
SWIGLU_LIMIT = 7.0
SWIGLU_ALPHA = 1.702
MOE_BLOCK = 128
Q_BLOCK = 128
ROPE_THETA = 10000.0
EPS = 1e-6
NEG_INF = -1e30

kernel_name = 'hybrid_diffattn_s5_stickbreaking_moe_step'


def rmsnorm(x, g):
    xf = x.astype(jnp.float32)
    y = xf * lax.rsqrt(jnp.mean(xf * xf, axis=-1, keepdims=True) + EPS)
    return (y * g.astype(jnp.float32)).astype(x.dtype)


def rope(x, pos):
    d = x.shape[-1]
    freqs = ROPE_THETA ** (-jnp.arange(0, d, 2, dtype=jnp.float32) / d)
    ang = pos.astype(jnp.float32)[:, None] * freqs[None, :]
    shape = (1, x.shape[1]) + (1,) * (x.ndim - 3) + (d // 2,)
    cos, sin = jnp.cos(ang).reshape(shape), jnp.sin(ang).reshape(shape)
    xf = x.astype(jnp.float32)
    x1, x2 = xf[..., : d // 2], xf[..., d // 2:]
    return jnp.concatenate([x1 * cos - x2 * sin, x2 * cos + x1 * sin], axis=-1).astype(x.dtype)


def ada_modulation(c, w, b):
    m = jax.nn.silu(c) @ w + b
    return jnp.split(m, 6, axis=-1)


def modulate(x, g, shift, scale):
    return rmsnorm(x, g) * (1.0 + scale[:, None, :]) + shift[:, None, :]


def gather_pages(cache, j, page_table):
    g = cache[j, page_table]
    return g.reshape((page_table.shape[0], -1) + cache.shape[3:])


def sweep_query_blocks(fn, q, q_pos):
    nb, t = q.shape[:2]
    blk = min(Q_BLOCK, t)
    n = t // blk
    qb = jnp.moveaxis(q.reshape((nb, n, blk) + q.shape[2:]), 1, 0)
    pb = q_pos.reshape(n, blk)
    ob = lax.map(lambda a: fn(a[0], a[1]), (qb, pb))
    ob = jnp.moveaxis(ob, 0, 1)
    return ob.reshape((nb, t) + ob.shape[3:])


def weighted_values(w, segs, eq):
    out = None
    off = 0
    for _, v, kp in segs:
        n = kp.shape[0]
        part = jnp.einsum(eq, w[..., off:off + n].astype(v.dtype), v)
        out = part if out is None else out + part
        off += n
    return out


def diff_qkv(h, w_in, q_gain, k_gain, pos):
    nb, t, _ = h.shape
    nqk = H_A * 2 * D_HA
    qkv = h @ w_in
    q = qkv[..., :nqk].reshape(nb, t, H_A, 2, D_HA)
    k = qkv[..., nqk:2 * nqk].reshape(nb, t, H_A, 2, D_HA)
    v = qkv[..., 2 * nqk:].reshape(nb, t, H_A, DV_A)
    q = rope(rmsnorm(q, q_gain), pos)
    k = rope(rmsnorm(k, k_gain), pos)
    return q, k, v


def diff_attention(q, q_pos, segs, lam):
    scale = D_HA ** -0.5
    s = jnp.concatenate([
        jnp.where(kp[None, :] <= q_pos[:, None],
                  jnp.einsum('bqhcd,bkhcd->bhcqk', q, k, preferred_element_type=jnp.float32) * scale,
                  NEG_INF)
        for k, _, kp in segs], axis=-1)
    p = jax.nn.softmax(s, axis=-1)
    a = p[:, :, 0] - lam * p[:, :, 1]
    return weighted_values(a, segs, 'bhqk,bkhv->bqhv')


def diff_output(o, sub_gain, w_out, lam_init):
    o = rmsnorm(o, sub_gain) * (1.0 - lam_init)
    return o.reshape(o.shape[0], o.shape[1], -1) @ w_out


def s5_mixer(u, h0_re, h0_im, log_dt, a_re, a_im, b_re, b_im, c_re, c_im, d_skip, w_glu, b_glu):
    f32 = jnp.float32
    nb, t, _ = u.shape
    uf = u.astype(f32).reshape(nb, t, SSM_G, SSM_CH)
    dt = jnp.exp(log_dt.astype(f32))[:, None]
    ar, ai = a_re.astype(f32), a_im.astype(f32)
    mag, ang = jnp.exp(dt * ar), dt * ai
    abr, abi = mag * jnp.cos(ang), mag * jnp.sin(ang)
    den = ar * ar + ai * ai
    fr = ((abr - 1.0) * ar + abi * ai) / den
    fi = (abi * ar - (abr - 1.0) * ai) / den
    br, bi = b_re.astype(f32), b_im.astype(f32)
    bbr = fr[..., None] * br - fi[..., None] * bi
    bbi = fr[..., None] * bi + fi[..., None] * br
    bu_r = jnp.einsum('btgc,gpc->btgp', uf, bbr)
    bu_i = jnp.einsum('btgc,gpc->btgp', uf, bbi)
    h0r, h0i = h0_re.astype(f32), h0_im.astype(f32)
    bu_r = bu_r.at[:, 0].add(abr * h0r - abi * h0i)
    bu_i = bu_i.at[:, 0].add(abr * h0i + abi * h0r)
    a_seq_r = jnp.broadcast_to(abr, bu_r.shape)
    a_seq_i = jnp.broadcast_to(abi, bu_i.shape)

    def combine(e1, e2):
        a1r, a1i, b1r, b1i = e1
        a2r, a2i, b2r, b2i = e2
        return (a2r * a1r - a2i * a1i, a2r * a1i + a2i * a1r,
                a2r * b1r - a2i * b1i + b2r, a2r * b1i + a2i * b1r + b2i)

    _, _, hr, hi = lax.associative_scan(combine, (a_seq_r, a_seq_i, bu_r, bu_i), axis=1)
    y = (jnp.einsum('btgp,gcp->btgc', hr, c_re.astype(f32))
         - jnp.einsum('btgp,gcp->btgc', hi, c_im.astype(f32)))
    y = y.reshape(nb, t, D_MODEL) + d_skip.astype(f32) * uf.reshape(nb, t, D_MODEL)
    y = jax.nn.gelu(y)
    z = y @ w_glu.astype(f32) + b_glu.astype(f32)
    out = z[..., :D_MODEL] * jax.nn.sigmoid(z[..., D_MODEL:])
    return out.astype(u.dtype), hr[:, -1], hi[:, -1]


def sb_qkv(h, w_in):
    nb, t, _ = h.shape
    q, k, v = jnp.split(h @ w_in, 3, axis=-1)
    shp = (nb, t, H_C, D_HC)
    return q.reshape(shp), k.reshape(shp), v.reshape(shp)


def stick_breaking_attention(q, q_pos, segs):
    scale = D_HC ** -0.5
    z = jnp.concatenate([
        jnp.einsum('bqhd,bkhd->bhqk', q, k, preferred_element_type=jnp.float32) * scale
        for k, _, _ in segs], axis=-1)
    k_pos = jnp.concatenate([kp for _, _, kp in segs], axis=0)
    mask = k_pos[None, :] < q_pos[:, None]
    log_stay = jnp.where(mask, jax.nn.log_sigmoid(-z), 0.0)
    log_later = lax.cumsum(log_stay, axis=z.ndim - 1, reverse=True) - log_stay
    w = jnp.where(mask, jnp.exp(jax.nn.log_sigmoid(z) + log_later), 0.0)
    return weighted_values(w, segs, 'bhqk,bkhd->bqhd')


def moe_ffn(h, w_router, b_router, w_gu, b_gu, w_down, b_down):
    shp = h.shape
    xt = h.reshape(-1, shp[-1])
    t = xt.shape[0]
    logits = xt.astype(jnp.float32) @ w_router.astype(jnp.float32) + b_router.astype(jnp.float32)
    top_val, top_idx = lax.top_k(logits, TOP_K)
    gates = jax.nn.softmax(top_val, axis=-1)
    n_assign = t * TOP_K
    flat_e = top_idx.reshape(-1)
    flat_tok = jnp.arange(n_assign, dtype=jnp.int32) // TOP_K
    flat_gate = gates.reshape(-1)
    order = jnp.argsort(flat_e)
    e_sorted, tok_sorted, gate_sorted = flat_e[order], flat_tok[order], flat_gate[order]
    counts = jnp.bincount(flat_e, length=N_EXPERTS)
    padded = (counts + MOE_BLOCK - 1) // MOE_BLOCK * MOE_BLOCK
    pad_end = jnp.cumsum(padded)
    pad_start = pad_end - padded
    start = jnp.cumsum(counts) - counts
    dest = pad_start[e_sorted] + jnp.arange(n_assign, dtype=jnp.int32) - start[e_sorted]
    n_blocks = -(-n_assign // MOE_BLOCK) + N_EXPERTS
    slot_tok = jnp.full((n_blocks * MOE_BLOCK,), t, jnp.int32).at[dest].set(tok_sorted)
    block_expert = jnp.minimum(
        jnp.searchsorted(pad_end, jnp.arange(n_blocks, dtype=jnp.int32) * MOE_BLOCK, side='right'),
        N_EXPERTS - 1)
    xpad = jnp.concatenate([xt, jnp.zeros((1, shp[-1]), xt.dtype)], axis=0)
    xb = xpad[slot_tok].reshape(n_blocks, MOE_BLOCK, shp[-1])

    def expert_block(args):
        xe, e = args
        gu = xe @ w_gu[e] + b_gu[e]
        gate, up = gu[:, :D_FF], gu[:, D_FF:]
        gate = jnp.minimum(gate, SWIGLU_LIMIT)
        up = jnp.clip(up, -SWIGLU_LIMIT, SWIGLU_LIMIT)
        act = (up + 1.0) * gate * jax.nn.sigmoid(SWIGLU_ALPHA * gate)
        return act @ w_down[e] + b_down[e]

    yb = lax.map(expert_block, (xb, block_expert)).reshape(-1, shp[-1])
    y = jnp.zeros((t, shp[-1]), h.dtype).at[tok_sorted].add(
        (yb[dest] * gate_sorted[:, None]).astype(h.dtype))
    return y.reshape(shp)


def setup_inputs(seed: int = 0) -> dict:
    key = jax.random.key(seed)
    k = jax.random.split(key, 40)
    f32 = jnp.float32

    def nrm(i, shape, s):
        return jax.random.normal(k[i], shape, f32) * s

    def gain(i, shape):
        return 1.0 + nrm(i, shape, 0.02)

    n_pages = PAST_LEN // PAGE_SIZE
    n_used = DEC_BATCH * n_pages
    n_pool = n_used + max(1, n_used // 4)
    page_table = jax.random.permutation(k[8], n_pool)[:n_used].reshape(DEC_BATCH, n_pages).astype(jnp.int32)
    d = D_MODEL
    nqk = H_A * 2 * D_HA
    a_im0 = jnp.pi * jnp.arange(SSM_P, dtype=f32)
    return {
        'x_prompt': nrm(0, (BATCH, SEQ, d), 1.0),
        'x_sample': nrm(1, (DEC_BATCH, DEC_SEQ, d), 1.0),
        'cache_a_k': nrm(2, (N_A, n_pool, PAGE_SIZE, H_A, 2, D_HA), 1.0),
        'cache_a_v': nrm(3, (N_A, n_pool, PAGE_SIZE, H_A, DV_A), 1.0),
        'state_b_re': nrm(4, (N_B, DEC_BATCH, SSM_G, SSM_P), 0.1),
        'state_b_im': nrm(5, (N_B, DEC_BATCH, SSM_G, SSM_P), 0.1),
        'cache_c_k': nrm(6, (N_C, n_pool, PAGE_SIZE, H_C, D_HC), 1.0),
        'cache_c_v': nrm(7, (N_C, n_pool, PAGE_SIZE, H_C, D_HC), 1.0),
        'page_table': page_table,
        'c_prompt': nrm(9, (BATCH, d), 1.0),
        'c_sample': nrm(10, (DEC_BATCH, d), 1.0),
        'w_ada': nrm(11, (DEPTH, d, 6 * d), 0.25 * d ** -0.5),
        'b_ada': nrm(12, (DEPTH, 6 * d), 0.02),
        'norm_mix': gain(13, (DEPTH, d)),
        'norm_ffn': gain(14, (DEPTH, d)),
        'a_w_in': nrm(15, (N_A, d, 2 * nqk + H_A * DV_A), d ** -0.5),
        'a_q_norm': gain(16, (N_A, D_HA)),
        'a_k_norm': gain(17, (N_A, D_HA)),
        'a_lambda': nrm(18, (N_A, 4, D_HA), 0.1),
        'a_sub_norm': gain(19, (N_A, DV_A)),
        'a_w_out': nrm(20, (N_A, H_A * DV_A, d), (H_A * DV_A) ** -0.5),
        'b_log_dt': jax.random.uniform(k[21], (N_B, SSM_G), f32, math.log(DT_MIN), math.log(DT_MAX)),
        'b_a_re': -0.5 + nrm(22, (N_B, SSM_G, SSM_P), 0.01),
        'b_a_im': a_im0 + nrm(23, (N_B, SSM_G, SSM_P), 0.01),
        'b_b_re': nrm(24, (N_B, SSM_G, SSM_P, SSM_CH), (2 * SSM_CH) ** -0.5),
        'b_b_im': nrm(25, (N_B, SSM_G, SSM_P, SSM_CH), (2 * SSM_CH) ** -0.5),
        'b_c_re': nrm(26, (N_B, SSM_G, SSM_CH, SSM_P), (2 * SSM_P) ** -0.5),
        'b_c_im': nrm(27, (N_B, SSM_G, SSM_CH, SSM_P), (2 * SSM_P) ** -0.5),
        'b_d': nrm(28, (N_B, d), 1.0),
        'b_w_glu': nrm(29, (N_B, d, 2 * d), d ** -0.5),
        'b_b_glu': nrm(30, (N_B, 2 * d), 0.02),
        'c_w_in': nrm(31, (N_C, d, 3 * H_C * D_HC), d ** -0.5),
        'c_w_out': nrm(32, (N_C, H_C * D_HC, d), (H_C * D_HC) ** -0.5),
        'moe_w_router': nrm(33, (DEPTH, d, N_EXPERTS), d ** -0.5),
        'moe_b_router': nrm(34, (DEPTH, N_EXPERTS), 0.01),
        'moe_w_gu': nrm(35, (DEPTH, N_EXPERTS, d, 2 * D_FF), d ** -0.5),
        'moe_b_gu': nrm(36, (DEPTH, N_EXPERTS, 2 * D_FF), 0.01),
        'moe_w_down': nrm(37, (DEPTH, N_EXPERTS, D_FF, d), D_FF ** -0.5),
        'moe_b_down': nrm(38, (DEPTH, N_EXPERTS, d), 0.01),
    }


def reference(x_prompt, x_sample, cache_a_k, cache_a_v, state_b_re, state_b_im,
              cache_c_k, cache_c_v, page_table, c_prompt, c_sample,
              w_ada, b_ada, norm_mix, norm_ffn,
              a_w_in, a_q_norm, a_k_norm, a_lambda, a_sub_norm, a_w_out,
              b_log_dt, b_a_re, b_a_im, b_b_re, b_b_im, b_c_re, b_c_im, b_d, b_w_glu, b_b_glu,
              c_w_in, c_w_out,
              moe_w_router, moe_b_router, moe_w_gu, moe_b_gu, moe_w_down, moe_b_down):
    seq = x_prompt.shape[1]
    dec_seq = x_sample.shape[1]
    past = page_table.shape[1] * cache_a_k.shape[2]
    pos_p = jnp.arange(seq, dtype=jnp.int32)
    pos_past = jnp.arange(past, dtype=jnp.int32)
    pos_s = past + jnp.arange(dec_seq, dtype=jnp.int32)
    zeros_state = jnp.zeros((x_prompt.shape[0], SSM_G, SSM_P), jnp.float32)

    xp, xs = x_prompt, x_sample
    new_a = ([], [], [], [])
    new_b = ([], [], [], [])
    new_c = ([], [], [], [])
    for i in range(DEPTH):
        kind, j = i % N_MIXERS, i // N_MIXERS
        mp = ada_modulation(c_prompt, w_ada[i], b_ada[i])
        ms = ada_modulation(c_sample, w_ada[i], b_ada[i])
        hp = modulate(xp, norm_mix[i], mp[0], mp[1])
        hs = modulate(xs, norm_mix[i], ms[0], ms[1])
        if kind == 0:
            lam_init = 0.8 - 0.6 * math.exp(-0.3 * i)
            lq = a_lambda[j].astype(jnp.float32)
            lam = jnp.exp(jnp.sum(lq[0] * lq[1])) - jnp.exp(jnp.sum(lq[2] * lq[3])) + lam_init
            qp, kp, vp = diff_qkv(hp, a_w_in[j], a_q_norm[j], a_k_norm[j], pos_p)
            op = sweep_query_blocks(
                lambda qb, pb: diff_attention(qb, pb, ((kp, vp, pos_p),), lam), qp, pos_p)
            qs, ks_new, vs_new = diff_qkv(hs, a_w_in[j], a_q_norm[j], a_k_norm[j], pos_s)
            segs = ((gather_pages(cache_a_k, j, page_table), gather_pages(cache_a_v, j, page_table), pos_past),
                    (ks_new, vs_new, pos_s))
            os_ = diff_attention(qs, pos_s, segs, lam)
            yp = diff_output(op, a_sub_norm[j], a_w_out[j], lam_init)
            ys = diff_output(os_, a_sub_norm[j], a_w_out[j], lam_init)
            for lst, arr in zip(new_a, (kp, vp, ks_new, vs_new)):
                lst.append(arr)
        elif kind == 1:
            ssm_w = (b_log_dt[j], b_a_re[j], b_a_im[j], b_b_re[j], b_b_im[j],
                     b_c_re[j], b_c_im[j], b_d[j], b_w_glu[j], b_b_glu[j])
            yp, hpr, hpi = s5_mixer(hp, zeros_state, zeros_state, *ssm_w)
            ys, hsr, hsi = s5_mixer(hs, state_b_re[j], state_b_im[j], *ssm_w)
            for lst, arr in zip(new_b, (hpr, hpi, hsr, hsi)):
                lst.append(arr)
        else:
            qp, kp, vp = sb_qkv(hp, c_w_in[j])
            op = sweep_query_blocks(
                lambda qb, pb: stick_breaking_attention(qb, pb, ((kp, vp, pos_p),)), qp, pos_p)
            qs, ks_new, vs_new = sb_qkv(hs, c_w_in[j])
            segs = ((gather_pages(cache_c_k, j, page_table), gather_pages(cache_c_v, j, page_table), pos_past),
                    (ks_new, vs_new, pos_s))
            os_ = stick_breaking_attention(qs, pos_s, segs)
            yp = op.reshape(op.shape[0], op.shape[1], -1) @ c_w_out[j]
            ys = os_.reshape(os_.shape[0], os_.shape[1], -1) @ c_w_out[j]
            for lst, arr in zip(new_c, (kp, vp, ks_new, vs_new)):
                lst.append(arr)
        xp = xp + mp[2][:, None, :] * yp
        xs = xs + ms[2][:, None, :] * ys
        moe_w = (moe_w_router[i], moe_b_router[i], moe_w_gu[i], moe_b_gu[i], moe_w_down[i], moe_b_down[i])
        xp = xp + mp[5][:, None, :] * moe_ffn(modulate(xp, norm_ffn[i], mp[3], mp[4]), *moe_w)
        xs = xs + ms[5][:, None, :] * moe_ffn(modulate(xs, norm_ffn[i], ms[3], ms[4]), *moe_w)

    return (xp, xs,
            jnp.stack(new_a[0]), jnp.stack(new_a[1]), jnp.stack(new_a[2]), jnp.stack(new_a[3]),
            jnp.stack(new_b[0]), jnp.stack(new_b[1]), jnp.stack(new_b[2]), jnp.stack(new_b[3]),
            jnp.stack(new_c[0]), jnp.stack(new_c[1]), jnp.stack(new_c[2]), jnp.stack(new_c[3]))
```

```python
import functools
import math

import jax
import jax.numpy as jnp
from jax import lax
from jax.experimental import pallas as pl
from jax.experimental.pallas import tpu as pltpu

F32 = jnp.float32
BF16 = jnp.bfloat16

EPS = 1e-6
NEG_INF = -1e30
ROPE_THETA = 10000.0
TOP_K = 4
SWIGLU_LIMIT = 7.0
SWIGLU_ALPHA = 1.702
N_MIXERS = 3
GROUP = 8
LANES = 128
VMEM_LIMIT = 56 * 1024 * 1024


def _params(sem, vmem=VMEM_LIMIT):
    return pltpu.CompilerParams(dimension_semantics=sem, vmem_limit_bytes=vmem)


def _pick(n, cands):
    for c in cands:
        if n % c == 0:
            return c
    raise ValueError(f"no tile for {n} in {cands}")


def _split2(x):
    hi = x.astype(BF16)
    lo = (x - hi.astype(F32)).astype(BF16)
    return hi, lo


def _sigmoid(x):
    return 1.0 / (1.0 + jnp.exp(-x))


def _dot(a, b):
    return jnp.dot(a, b, preferred_element_type=F32)


def _dot_nt(a, b):
    return lax.dot_general(a, b, (((1,), (1,)), ((), ())), preferred_element_type=F32)


def _mm_kernel(*refs, norm, silu_in, glu, has_bias, resid, passes, bb, rows):
    it = iter(refs)
    x_ref = next(it)
    if norm:
        g_ref, sh_ref, sc_ref = next(it), next(it), next(it)
    w_refs = [next(it)] + ([next(it)] if glu else [])
    b_refs = ([next(it)] + ([next(it)] if glu else [])) if has_bias else []
    if resid:
        res_ref, gate_ref = next(it), next(it)
    o_ref = next(it)
    whi = [next(it) for _ in w_refs]
    wlo = [next(it) for _ in w_refs] if passes == 3 else []

    @pl.when(pl.program_id(1) == 0)
    def _():
        for i, w_ref in enumerate(w_refs):
            w = w_ref[...]
            hi = w.astype(BF16)
            whi[i][...] = hi
            if passes == 3:
                wlo[i][...] = (w - hi.astype(F32)).astype(BF16)

    x = x_ref[...]
    if silu_in:
        x = x * _sigmoid(x)
    if norm:
        ms = jnp.mean(x * x, axis=-1, keepdims=True)
        x = x * lax.rsqrt(ms + EPS) * g_ref[...]
        x = x * (1.0 + sc_ref[...]) + sh_ref[...]
    x2 = x.reshape(bb * rows, x.shape[-1])
    xh = x2.astype(BF16)
    xl = (x2 - xh.astype(F32)).astype(BF16) if passes == 3 else None

    def mm(i):
        acc = _dot(xh, whi[i][...])
        if passes == 3:
            acc = acc + _dot(xl, whi[i][...]) + _dot(xh, wlo[i][...])
        if has_bias:
            acc = acc + b_refs[i][...]
        return acc

    z = mm(0)
    if glu:
        z = z * _sigmoid(mm(1))
    z3 = z.reshape(bb, rows, z.shape[-1])
    if resid:
        z3 = res_ref[...] + gate_ref[...] * z3
    o_ref[...] = z3


def fused_matmul(x3, w, b=None, *, layer=0, norm=None, silu_in=False, glu=False,
                 resid=None, passes=1, tn=None, name="mm"):
    G, R, K = x3.shape
    N = w.shape[-1]
    n_out = N // 2 if glu else N
    bb = _pick(G, (64, 32, 16, 8, 4, 2, 1))
    tn = tn or min(n_out, 1024)
    assert n_out % tn == 0
    half = n_out // tn

    in_specs = [pl.BlockSpec((bb, R, K), lambda n, g: (g, 0, 0))]
    args = [x3]
    if norm is not None:
        gain, gl, modexp, i_sh, i_sc = norm
        gain3 = gain.reshape(gain.shape[0], 1, K)
        in_specs += [pl.BlockSpec((None, 1, K), lambda n, g: (gl, 0, 0)),
                     pl.BlockSpec((None, bb, 1, K), lambda n, g: (i_sh, g, 0, 0)),
                     pl.BlockSpec((None, bb, 1, K), lambda n, g: (i_sc, g, 0, 0))]
        args += [gain3, modexp, modexp]
    in_specs.append(pl.BlockSpec((None, K, tn), lambda n, g: (layer, 0, n)))
    args.append(w)
    if glu:
        in_specs.append(pl.BlockSpec((None, K, tn), lambda n, g: (layer, 0, n + half)))
        args.append(w)
    if b is not None:
        b3 = b.reshape(b.shape[0], 1, N)
        in_specs.append(pl.BlockSpec((None, 1, tn), lambda n, g: (layer, 0, n)))
        args.append(b3)
        if glu:
            in_specs.append(pl.BlockSpec((None, 1, tn), lambda n, g: (layer, 0, n + half)))
            args.append(b3)
    if resid is not None:
        xres, modexp_r, i_g = resid
        in_specs += [pl.BlockSpec((bb, R, tn), lambda n, g: (g, 0, n)),
                     pl.BlockSpec((None, bb, 1, tn), lambda n, g: (i_g, g, 0, n))]
        args += [xres, modexp_r]
    nw = 2 if glu else 1
    scratch = [pltpu.VMEM((K, tn), BF16) for _ in range(nw * (2 if passes == 3 else 1))]
    kern = functools.partial(_mm_kernel, norm=norm is not None, silu_in=silu_in, glu=glu,
                             has_bias=b is not None, resid=resid is not None, passes=passes,
                             bb=bb, rows=R)
    return pl.pallas_call(
        kern,
        grid=(n_out // tn, G // bb),
        in_specs=in_specs,
        out_specs=pl.BlockSpec((bb, R, tn), lambda n, g: (g, 0, n)),
        out_shape=jax.ShapeDtypeStruct((G, R, n_out), F32),
        scratch_shapes=scratch,
        compiler_params=_params(("arbitrary", "arbitrary")),
        name=name,
    )(*args)


def _mod_norm_kernel(x_ref, g_ref, sh_ref, sc_ref, o_ref):
    x = x_ref[...]
    ms = jnp.mean(x * x, axis=-1, keepdims=True)
    x = x * lax.rsqrt(ms + EPS) * g_ref[...]
    o_ref[...] = x * (1.0 + sc_ref[...]) + sh_ref[...]


def mod_norm(x3, gain, layer, modexp, i_sh, i_sc):
    G, R, K = x3.shape
    bb = _pick(G, (64, 32, 16, 8, 4, 2, 1))
    return pl.pallas_call(
        _mod_norm_kernel,
        grid=(G // bb,),
        in_specs=[pl.BlockSpec((bb, R, K), lambda g: (g, 0, 0)),
                  pl.BlockSpec((None, 1, K), lambda g: (layer, 0, 0)),
                  pl.BlockSpec((None, bb, 1, K), lambda g: (i_sh, g, 0, 0)),
                  pl.BlockSpec((None, bb, 1, K), lambda g: (i_sc, g, 0, 0))],
        out_specs=pl.BlockSpec((bb, R, K), lambda g: (g, 0, 0)),
        out_shape=jax.ShapeDtypeStruct((G, R, K), F32),
        compiler_params=_params(("parallel",)),
        name="mod_norm",
    )(x3, gain.reshape(gain.shape[0], 1, K), modexp, modexp)


def _qknorm_rope_kernel(x_ref, g_ref, cos_ref, sin_ref, o_ref, *, d_head):
    x = x_ref[...]
    r = lax.broadcasted_iota(jnp.int32, (LANES, LANES), 0) // d_head
    c = lax.broadcasted_iota(jnp.int32, (LANES, LANES), 1) // d_head
    seg = jnp.where(r == c, 1.0, 0.0).astype(BF16)
    hi, lo = _split2(x * x)
    ss = _dot(hi, seg) + _dot(lo, seg)
    y = x * lax.rsqrt(ss * (1.0 / d_head) + EPS) * g_ref[...]
    lane = lax.broadcasted_iota(jnp.int32, x.shape, 1)
    half = d_head // 2
    rot = jnp.where(lane % d_head < half, pltpu.roll(y, LANES - half, axis=1), pltpu.roll(y, half, axis=1))
    o_ref[...] = y * cos_ref[...] + rot * sin_ref[...]


def qknorm_rope(qkv2, gains, cos_t, sin_t, n_qk_cols, d_head):
    T = qkv2.shape[0]
    tm = _pick(T, (1024, 512, 256, 128, 64, 32, 16, 8))
    ncb = n_qk_cols // LANES
    per = ncb // 2
    return pl.pallas_call(
        functools.partial(_qknorm_rope_kernel, d_head=d_head),
        grid=(T // tm, ncb),
        in_specs=[pl.BlockSpec((tm, LANES), lambda i, j: (i, j)),
                  pl.BlockSpec((None, 1, LANES), lambda i, j: (j // per, 0, 0)),
                  pl.BlockSpec((tm, LANES), lambda i, j: (i, 0)),
                  pl.BlockSpec((tm, LANES), lambda i, j: (i, 0))],
        out_specs=pl.BlockSpec((tm, LANES), lambda i, j: (i, j)),
        out_shape=jax.ShapeDtypeStruct((T, n_qk_cols), F32),
        compiler_params=_params(("parallel", "parallel")),
        name="qknorm_rope",
    )(qkv2, gains, cos_t, sin_t)


def _lambda_value(alam_ref, lam_init):
    a = alam_ref[...]
    s1 = jnp.sum(a[0:1] * a[1:2], axis=1, keepdims=True)
    s2 = jnp.sum(a[2:3] * a[3:4], axis=1, keepdims=True)
    return jnp.exp(s1) - jnp.exp(s2) + lam_init


def _softmax_update(s, v_bf, m_ref, l_ref, acc_ref, reps_s, reps_acc):
    m_prev = m_ref[...]
    m_new = jnp.maximum(m_prev, jnp.max(s, axis=1, keepdims=True))
    alpha = jnp.exp(m_prev - m_new)
    m_b = m_new if reps_s == 1 else jnp.concatenate([m_new] * reps_s, axis=1)
    p = jnp.exp(s - m_b)
    l_ref[...] = alpha * l_ref[...] + jnp.sum(p, axis=1, keepdims=True)
    a_b = alpha if reps_acc == 1 else jnp.concatenate([alpha] * reps_acc, axis=1)
    acc_ref[...] = a_b * acc_ref[...] + _dot(p.astype(BF16), v_bf)
    m_ref[...] = m_new


def _diff_prompt_kernel(q_ref, k_ref, v_ref, g_ref, alam_ref, o_ref,
                        kbf, vbf, m0, l0, a0, m1, l1, a1, *, tq, d_head, lam_init):
    qi = pl.program_id(2)

    @pl.when(qi == 0)
    def _():
        kbf[...] = k_ref[...].astype(BF16)
        vbf[...] = v_ref[...].astype(BF16)

    q = q_ref[...]
    lane = lax.broadcasted_iota(jnp.int32, q.shape, 1)
    qc = (jnp.where(lane < d_head, q, 0.0).astype(BF16), jnp.where(lane >= d_head, q, 0.0).astype(BF16))
    stats = ((m0, l0, a0), (m1, l1, a1))
    for m, l, a in stats:
        m[...] = jnp.full(m.shape, NEG_INF, F32)
        l[...] = jnp.zeros(l.shape, F32)
        a[...] = jnp.zeros(a.shape, F32)
    reps = tq // LANES

    def chunk(kj, masked):
        off = pl.multiple_of(kj * tq, tq)
        kb = kbf[pl.ds(off, tq), :]
        vb = vbf[pl.ds(off, tq), :]
        for c in range(2):
            s = _dot_nt(qc[c], kb)
            if masked:
                row = lax.broadcasted_iota(jnp.int32, s.shape, 0)
                col = lax.broadcasted_iota(jnp.int32, s.shape, 1)
                s = jnp.where(col <= row, s, NEG_INF)
            _softmax_update(s, vb, *stats[c], reps, 1)

    def body(kj, carry):
        chunk(kj, False)
        return carry

    lax.fori_loop(0, qi, body, 0)
    chunk(qi, True)

    lam = _lambda_value(alam_ref, lam_init)
    o = a0[...] / l0[...] - lam * (a1[...] / l1[...])
    ms = jnp.mean(o * o, axis=1, keepdims=True)
    o_ref[...] = o * lax.rsqrt(ms + EPS) * g_ref[...] * (1.0 - lam_init)


def diff_attention_prompt(qk2, qkv2, sub_gain, alam, *, batch, seq, n_heads, d_head, lam_init, layer):
    T = qk2.shape[0]
    dv = 2 * d_head
    assert dv == LANES
    tq = _pick(seq, (256, 128))
    nq = seq // tq
    kcol = n_heads
    vcol = 2 * n_heads
    kern = functools.partial(_diff_prompt_kernel, tq=tq, d_head=d_head, lam_init=lam_init)
    return pl.pallas_call(
        kern,
        grid=(batch, n_heads, nq),
        in_specs=[pl.BlockSpec((tq, LANES), lambda b, h, i: (b * nq + i, h)),
                  pl.BlockSpec((seq, LANES), lambda b, h, i: (b, kcol + h)),
                  pl.BlockSpec((seq, LANES), lambda b, h, i: (b, vcol + h)),
                  pl.BlockSpec((None, 1, LANES), lambda b, h, i: (layer, 0, 0)),
                  pl.BlockSpec((None, 4, d_head), lambda b, h, i: (layer, 0, 0))],
        out_specs=pl.BlockSpec((tq, LANES), lambda b, h, i: (b * nq + i, h)),
        out_shape=jax.ShapeDtypeStruct((T, n_heads * dv), F32),
        scratch_shapes=[pltpu.VMEM((seq, LANES), BF16), pltpu.VMEM((seq, LANES), BF16)]
        + [pltpu.VMEM((tq, LANES), F32) for _ in range(6)],
        compiler_params=_params(("arbitrary", "arbitrary", "arbitrary")),
        name="diff_attn_prompt",
    )(qk2, qk2, qkv2, sub_gain.reshape(sub_gain.shape[0], 1, dv), alam)


def _diff_sample_kernel(pt_ref, q_ref, kn_ref, vn_ref, kp_ref, vp_ref, g_ref, alam_ref, oin_ref, o_ref,
                        qbd, m, l, acc, *, n_pages, n_heads, d_head, nq, lam_init):
    del pt_ref, oin_ref
    p = pl.program_id(1)
    dm = n_heads * 2 * d_head
    nrow = 2 * n_heads * nq
    row = lax.broadcasted_iota(jnp.int32, (nrow, dm), 0)
    col = lax.broadcasted_iota(jnp.int32, (nrow, dm), 1)
    comp = row // (n_heads * nq)
    head = (row // nq) % n_heads

    @pl.when(p == 0)
    def _():
        qt = jnp.concatenate([q_ref[...]] * (2 * n_heads), axis=0)
        qbd[...] = jnp.where(col // d_head == head * 2 + comp, qt, 0.0).astype(BF16)
        m[...] = jnp.full(m.shape, NEG_INF, F32)
        l[...] = jnp.zeros(l.shape, F32)
        acc[...] = jnp.zeros(acc.shape, F32)

    def update(kb, vb, mask):
        s = _dot_nt(qbd[...], kb)
        if mask is not None:
            s = jnp.where(mask, s, NEG_INF)
        _softmax_update(s, vb, m, l, acc, s.shape[1] // LANES, dm // LANES)

    @pl.when(p < n_pages)
    def _():
        update(kp_ref[...].astype(BF16), vp_ref[...].astype(BF16), None)

    @pl.when(p == n_pages)
    def _():
        pad = jnp.zeros((LANES - nq, dm), F32)
        kn = jnp.concatenate([kn_ref[...], pad], axis=0).astype(BF16)
        vn = jnp.concatenate([vn_ref[...], pad], axis=0).astype(BF16)
        r = lax.broadcasted_iota(jnp.int32, (nrow, LANES), 0)
        j = lax.broadcasted_iota(jnp.int32, (nrow, LANES), 1)
        update(kn, vn, j <= r % nq)
        lam = _lambda_value(alam_ref, lam_init)
        l_b = jnp.concatenate([l[...]] * (dm // LANES), axis=1)
        own = col // (2 * d_head) == head
        wgt = jnp.where(own, jnp.where(comp == 0, 1.0, -lam), 0.0)
        comb = acc[...] / l_b * wgt
        o = comb[0:nq]
        for i in range(1, 2 * n_heads):
            o = o + comb[i * nq:(i + 1) * nq]
        dv = 2 * d_head
        for h in range(n_heads):
            oh = o[:, h * dv:(h + 1) * dv]
            ms = jnp.mean(oh * oh, axis=1, keepdims=True)
            o_ref[:, h * dv:(h + 1) * dv] = oh * lax.rsqrt(ms + EPS) * g_ref[...] * (1.0 - lam_init)


def diff_attention_sample(o_prompt, qk2, qkv2, cache_k, cache_v, page_table, sub_gain, alam, *,
                          n_prompt_rows, n_heads, d_head, nq, lam_init, layer):
    T, dm = o_prompt.shape
    nseq, n_pages = page_table.shape
    n_pool, page = cache_k.shape[1], cache_k.shape[2]
    assert page == LANES and 2 * n_heads * nq == LANES and nq == GROUP
    ck = cache_k.reshape(cache_k.shape[0], n_pool, page, dm)
    cv = cache_v.reshape(cache_v.shape[0], n_pool, page, dm)
    r0 = n_prompt_rows // nq
    last = n_pages - 1

    def page_map(b, p, pt):
        return (layer, pt[b * n_pages + jnp.minimum(p, last)], 0, 0)

    grid_spec = pltpu.PrefetchScalarGridSpec(
        num_scalar_prefetch=1,
        grid=(nseq, n_pages + 1),
        in_specs=[pl.BlockSpec((nq, dm), lambda b, p, pt: (r0 + b, 0)),
                  pl.BlockSpec((nq, dm), lambda b, p, pt: (r0 + b, 1)),
                  pl.BlockSpec((nq, dm), lambda b, p, pt: (r0 + b, 2)),
                  pl.BlockSpec((None, None, page, dm), page_map),
                  pl.BlockSpec((None, None, page, dm), page_map),
                  pl.BlockSpec((None, 1, 2 * d_head), lambda b, p, pt: (layer, 0, 0)),
                  pl.BlockSpec((None, 4, d_head), lambda b, p, pt: (layer, 0, 0)),
                  pl.BlockSpec(memory_space=pl.ANY)],
        out_specs=pl.BlockSpec((nq, dm), lambda b, p, pt: (r0 + b, 0)),
        scratch_shapes=[pltpu.VMEM((LANES, dm), BF16), pltpu.VMEM((LANES, LANES), F32),
                        pltpu.VMEM((LANES, LANES), F32), pltpu.VMEM((LANES, dm), F32)],
    )
    kern = functools.partial(_diff_sample_kernel, n_pages=n_pages, n_heads=n_heads, d_head=d_head,
                             nq=nq, lam_init=lam_init)
    return pl.pallas_call(
        kern,
        grid_spec=grid_spec,
        out_shape=jax.ShapeDtypeStruct((T, dm), F32),
        input_output_aliases={8: 0},
        compiler_params=_params(("arbitrary", "arbitrary")),
        name="diff_attn_sample",
    )(page_table.reshape(-1), qk2, qk2, qkv2, ck, cv,
      sub_gain.reshape(sub_gain.shape[0], 1, 2 * d_head), alam, o_prompt)


def _later_matrix():
    j = lax.broadcasted_iota(jnp.int32, (LANES, LANES), 0)
    s = lax.broadcasted_iota(jnp.int32, (LANES, LANES), 1)
    return jnp.where(j > s, 1.0, 0.0).astype(BF16)


def _stick_weights(z, carry, mask, later):
    log_stay = -(jnp.maximum(z, 0.0) + jnp.log(1.0 + jnp.exp(-jnp.abs(z))))
    if mask is not None:
        log_stay = jnp.where(mask, log_stay, 0.0)
    hi, lo = _split2(log_stay)
    log_later = _dot(hi, later) + _dot(lo, later) + carry
    w = jnp.exp(z + log_stay + log_later)
    if mask is not None:
        w = jnp.where(mask, w, 0.0)
    return w, carry + jnp.sum(log_stay, axis=1, keepdims=True)


def _sb_prompt_kernel(q_ref, k_ref, v_ref, o_ref, kbf, vbf, carry, acc, *, tq, d_head, scale):
    qi = pl.program_id(2)

    @pl.when(qi == 0)
    def _():
        kbf[...] = k_ref[...].astype(BF16)
        vbf[...] = v_ref[...].astype(BF16)

    q = q_ref[...] * scale
    lane = lax.broadcasted_iota(jnp.int32, q.shape, 1)
    first = lane < d_head
    qh = (jnp.where(first, q, 0.0).astype(BF16), jnp.where(first, 0.0, q).astype(BF16))
    later = _later_matrix()
    nsub = tq // LANES
    out = []
    for h in range(2):
        carry[...] = jnp.zeros(carry.shape, F32)
        acc[...] = jnp.zeros(acc.shape, F32)

        def sub(kc, masked_rows_from, h=h):
            off = pl.multiple_of(kc * LANES, LANES)
            kb = kbf[pl.ds(off, LANES), :]
            vb = vbf[pl.ds(off, LANES), :]
            z = _dot_nt(qh[h], kb)
            mask = None
            if masked_rows_from is not None:
                row = lax.broadcasted_iota(jnp.int32, z.shape, 0)
                col = lax.broadcasted_iota(jnp.int32, z.shape, 1)
                mask = col + masked_rows_from * LANES < row
            w, c_new = _stick_weights(z, carry[...], mask, later)
            acc[...] += _dot(w.astype(BF16), vb)
            carry[...] = c_new

        for jd in reversed(range(nsub)):
            sub(qi * nsub + jd, jd)

        def body(t, c):
            sub(qi * nsub - 1 - t, None)
            return c

        lax.fori_loop(0, qi * nsub, body, 0)
        out.append(acc[...])
    o_ref[...] = jnp.where(first, out[0], out[1])


def sb_attention_prompt(qkv2, *, batch, seq, n_heads, d_head):
    T = qkv2.shape[0]
    dm = n_heads * d_head
    tq = _pick(seq, (256, 128))
    nq = seq // tq
    npair = dm // LANES
    kern = functools.partial(_sb_prompt_kernel, tq=tq, d_head=d_head, scale=d_head ** -0.5)
    return pl.pallas_call(
        kern,
        grid=(batch, npair, nq),
        in_specs=[pl.BlockSpec((tq, LANES), lambda b, h, i: (b * nq + i, h)),
                  pl.BlockSpec((seq, LANES), lambda b, h, i: (b, npair + h)),
                  pl.BlockSpec((seq, LANES), lambda b, h, i: (b, 2 * npair + h))],
        out_specs=pl.BlockSpec((tq, LANES), lambda b, h, i: (b * nq + i, h)),
        out_shape=jax.ShapeDtypeStruct((T, dm), F32),
        scratch_shapes=[pltpu.VMEM((seq, LANES), BF16), pltpu.VMEM((seq, LANES), BF16),
                        pltpu.VMEM((tq, LANES), F32), pltpu.VMEM((tq, LANES), F32)],
        compiler_params=_params(("arbitrary", "arbitrary", "arbitrary")),
        name="sb_attn_prompt",
    )(qkv2, qkv2, qkv2)


def _sb_sample_kernel(pt_ref, q_ref, kn_ref, vn_ref, kp_ref, vp_ref, oin_ref, o_ref,
                      qbd, carry, acc, *, n_pages, n_heads, d_head, nq, scale):
    del pt_ref, oin_ref
    p = pl.program_id(1)
    dm = n_heads * d_head
    nrow = n_heads * nq
    row = lax.broadcasted_iota(jnp.int32, (nrow, dm), 0)
    col = lax.broadcasted_iota(jnp.int32, (nrow, dm), 1)
    own = col // d_head == row // nq
    later = _later_matrix()

    def update(kb, vb, mask):
        z = _dot_nt(qbd[...], kb)
        w, c_new = _stick_weights(z, carry[...], mask, later)
        acc[...] += _dot(w.astype(BF16), vb)
        carry[...] = c_new

    @pl.when(p == 0)
    def _():
        qt = jnp.concatenate([q_ref[...] * scale] * n_heads, axis=0)
        qbd[...] = jnp.where(own, qt, 0.0).astype(BF16)
        carry[...] = jnp.zeros(carry.shape, F32)
        acc[...] = jnp.zeros(acc.shape, F32)
        pad = jnp.zeros((LANES - nq, dm), F32)
        kn = jnp.concatenate([kn_ref[...], pad], axis=0).astype(BF16)
        vn = jnp.concatenate([vn_ref[...], pad], axis=0).astype(BF16)
        r = lax.broadcasted_iota(jnp.int32, (nrow, LANES), 0)
        j = lax.broadcasted_iota(jnp.int32, (nrow, LANES), 1)
        update(kn, vn, j < r % nq)

    @pl.when(p > 0)
    def _():
        update(kp_ref[...].astype(BF16), vp_ref[...].astype(BF16), None)

    @pl.when(p == n_pages)
    def _():
        comb = jnp.where(own, acc[...], 0.0)
        o = comb[0:nq]
        for i in range(1, n_heads):
            o = o + comb[i * nq:(i + 1) * nq]
        o_ref[...] = o


def sb_attention_sample(o_prompt, qkv2, cache_k, cache_v, page_table, *, n_prompt_rows, n_heads, d_head, nq, layer):
    T, dm = o_prompt.shape
    nseq, n_pages = page_table.shape
    n_pool, page = cache_k.shape[1], cache_k.shape[2]
    assert page == LANES and n_heads * nq == LANES and nq == GROUP
    ck = cache_k.reshape(cache_k.shape[0], n_pool, page, dm)
    cv = cache_v.reshape(cache_v.shape[0], n_pool, page, dm)
    r0 = n_prompt_rows // nq

    def page_map(b, p, pt):
        return (layer, pt[b * n_pages + n_pages - jnp.maximum(p, 1)], 0, 0)

    grid_spec = pltpu.PrefetchScalarGridSpec(
        num_scalar_prefetch=1,
        grid=(nseq, n_pages + 1),
        in_specs=[pl.BlockSpec((nq, dm), lambda b, p, pt: (r0 + b, 0)),
                  pl.BlockSpec((nq, dm), lambda b, p, pt: (r0 + b, 1)),
                  pl.BlockSpec((nq, dm), lambda b, p, pt: (r0 + b, 2)),
                  pl.BlockSpec((None, None, page, dm), page_map),
                  pl.BlockSpec((None, None, page, dm), page_map),
                  pl.BlockSpec(memory_space=pl.ANY)],
        out_specs=pl.BlockSpec((nq, dm), lambda b, p, pt: (r0 + b, 0)),
        scratch_shapes=[pltpu.VMEM((LANES, dm), BF16), pltpu.VMEM((LANES, LANES), F32),
                        pltpu.VMEM((LANES, dm), F32)],
    )
    kern = functools.partial(_sb_sample_kernel, n_pages=n_pages, n_heads=n_heads, d_head=d_head,
                             nq=nq, scale=d_head ** -0.5)
    return pl.pallas_call(
        kern,
        grid_spec=grid_spec,
        out_shape=jax.ShapeDtypeStruct((T, dm), F32),
        input_output_aliases={6: 0},
        compiler_params=_params(("arbitrary", "arbitrary")),
        name="sb_attn_sample",
    )(page_table.reshape(-1), qkv2, qkv2, qkv2, ck, cv, o_prompt)


def _gelu_tanh(x):
    return 0.5 * x * (1.0 + jnp.tanh(math.sqrt(2.0 / math.pi) * (x + 0.044715 * x * x * x)))


def _s5_kernel(u_ref, h0r_ref, h0i_ref, wb_ref, ar_ref, ai_ref, wc_ref, d_ref,
               y_ref, hr_ref, hi_ref,
               bur, bui, str_, sti, wbhi, wblo, wcbf, *, nb, tt, n_chunk, slabs_per_step):
    t_idx = pl.program_id(1)
    rows = nb * tt
    dm = u_ref.shape[-1]
    ck = dm // n_chunk
    n_slab = bur.shape[0]
    spc = n_slab // n_chunk
    sk = spc * LANES

    @pl.when((pl.program_id(0) == 0) & (t_idx == 0))
    def _():
        wb = wb_ref[...]
        hi = wb.astype(BF16)
        wbhi[...] = hi
        wblo[...] = (wb - hi.astype(F32)).astype(BF16)
        wcbf[...] = wc_ref[...].astype(BF16)

    @pl.when(t_idx == 0)
    def _():
        str_[...] = h0r_ref[...]
        sti[...] = h0i_ref[...]

    u2 = u_ref[...].reshape(rows, dm)
    for j in range(n_chunk):
        uj = u2[:, j * ck:(j + 1) * ck]
        uh, ul = _split2(uj)
        bu = _dot(uh, wbhi[j]) + _dot(ul, wbhi[j]) + _dot(uh, wblo[j])
        for q in range(spc):
            bur[j * spc + q] = bu[:, q * LANES:(q + 1) * LANES]
            bui[j * spc + q] = bu[:, sk + q * LANES:sk + (q + 1) * LANES]

    rc = min(nb, GROUP)
    for r0 in range(0, nb, rc):
        for s0 in range(0, n_slab, slabs_per_step):
            sl = [slice((s0 + q) * LANES, (s0 + q + 1) * LANES) for q in range(slabs_per_step)]
            ar = [ar_ref[:, c] for c in sl]
            ai = [ai_ref[:, c] for c in sl]
            h0 = tuple(str_[r0:r0 + rc, c] for c in sl) + tuple(sti[r0:r0 + rc, c] for c in sl)

            def step(t, h, r0=r0, s0=s0, ar=ar, ai=ai):
                rows_t = pl.ds(r0 * tt + t, rc, stride=tt)
                out_r, out_i = [], []
                for q in range(slabs_per_step):
                    h_r, h_i = h[q], h[slabs_per_step + q]
                    n_r = ar[q] * h_r - ai[q] * h_i + bur[s0 + q, rows_t, :]
                    n_i = ar[q] * h_i + ai[q] * h_r + bui[s0 + q, rows_t, :]
                    bur[s0 + q, rows_t, :] = n_r
                    bui[s0 + q, rows_t, :] = n_i
                    out_r.append(n_r)
                    out_i.append(n_i)
                return tuple(out_r) + tuple(out_i)

            h = lax.fori_loop(0, tt, step, h0)
            for q, c in enumerate(sl):
                str_[r0:r0 + rc, c] = h[q]
                sti[r0:r0 + rc, c] = h[slabs_per_step + q]

    hr_ref[...] = str_[...]
    hi_ref[...] = sti[...]

    for j in range(n_chunk):
        hcat = jnp.concatenate([bur[j * spc + q] for q in range(spc)] + [bui[j * spc + q] for q in range(spc)], axis=1)
        yj = _dot(hcat.astype(BF16), wcbf[j]) + d_ref[:, j * ck:(j + 1) * ck] * u2[:, j * ck:(j + 1) * ck]
        y_ref[:, :, j * ck:(j + 1) * ck] = _gelu_tanh(yj).reshape(nb, tt, ck)


def s5_scan(u3, h0r, h0i, wb, ar, ai, wc, d_skip, *, nb, tt, name):
    B, T, D = u3.shape
    S = h0r.shape[1]
    n_chunk = wb.shape[0]
    kern = functools.partial(_s5_kernel, nb=nb, tt=tt, n_chunk=n_chunk, slabs_per_step=4)
    full = lambda a: pl.BlockSpec(a.shape, lambda b, t: (0,) * a.ndim)
    return pl.pallas_call(
        kern,
        grid=(B // nb, T // tt),
        in_specs=[pl.BlockSpec((nb, tt, D), lambda b, t: (b, t, 0)),
                  pl.BlockSpec((nb, S), lambda b, t: (b, 0)),
                  pl.BlockSpec((nb, S), lambda b, t: (b, 0)),
                  full(wb), full(ar), full(ai), full(wc), full(d_skip)],
        out_specs=[pl.BlockSpec((nb, tt, D), lambda b, t: (b, t, 0)),
                   pl.BlockSpec((nb, S), lambda b, t: (b, 0)),
                   pl.BlockSpec((nb, S), lambda b, t: (b, 0))],
        out_shape=[jax.ShapeDtypeStruct((B, T, D), F32),
                   jax.ShapeDtypeStruct((B, S), F32),
                   jax.ShapeDtypeStruct((B, S), F32)],
        scratch_shapes=[pltpu.VMEM((S // LANES, nb * tt, LANES), F32), pltpu.VMEM((S // LANES, nb * tt, LANES), F32),
                        pltpu.VMEM((nb, S), F32), pltpu.VMEM((nb, S), F32),
                        pltpu.VMEM(wb.shape, BF16), pltpu.VMEM(wb.shape, BF16), pltpu.VMEM(wc.shape, BF16)],
        compiler_params=_params(("arbitrary", "arbitrary")),
        name=name,
    )(u3, h0r, h0i, wb, ar, ai, wc, d_skip)


def s5_tables(log_dt, a_re, a_im, b_re, b_im, c_re, c_im, n_chunk):
    ng, p = a_re.shape
    ch = b_re.shape[-1]
    dt = jnp.exp(log_dt)[:, None]
    mag, ang = jnp.exp(dt * a_re), dt * a_im
    abr, abi = mag * jnp.cos(ang), mag * jnp.sin(ang)
    den = a_re * a_re + a_im * a_im
    fr = ((abr - 1.0) * a_re + abi * a_im) / den
    fi = (abi * a_re - (abr - 1.0) * a_im) / den
    bbr = fr[..., None] * b_re - fi[..., None] * b_im
    bbi = fr[..., None] * b_im + fi[..., None] * b_re
    gpc = ng // n_chunk
    eye = jnp.eye(gpc, dtype=F32)

    def in_proj(bb):
        b4 = bb.reshape(n_chunk, gpc, p, ch)
        return jnp.einsum("jgpc,gh->jgchp", b4, eye).reshape(n_chunk, gpc * ch, gpc * p)

    def out_proj(cc):
        c4 = cc.reshape(n_chunk, gpc, ch, p)
        return jnp.einsum("jgcp,gh->jgphc", c4, eye).reshape(n_chunk, gpc * p, gpc * ch)

    wb = jnp.concatenate([in_proj(bbr), in_proj(bbi)], axis=2)
    wc = jnp.concatenate([out_proj(c_re), -out_proj(c_im)], axis=1)
    return wb, abr.reshape(1, ng * p), abi.reshape(1, ng * p), wc


def _router_kernel(x_ref, g_ref, sh_ref, sc_ref, w_ref, b_ref, h_ref, gate_ref, idx_ref, whi, wlo, *, bb, rows):
    @pl.when(pl.program_id(0) == 0)
    def _():
        w = w_ref[...]
        hi = w.astype(BF16)
        whi[...] = hi
        wlo[...] = (w - hi.astype(F32)).astype(BF16)

    x = x_ref[...]
    ms = jnp.mean(x * x, axis=-1, keepdims=True)
    x = x * lax.rsqrt(ms + EPS) * g_ref[...]
    x = x * (1.0 + sc_ref[...]) + sh_ref[...]
    h_ref[...] = x
    x2 = x.reshape(bb * rows, x.shape[-1])
    xh, xl = _split2(x2)
    logits = _dot(xh, whi[...]) + _dot(xl, whi[...]) + _dot(xh, wlo[...]) + b_ref[...]
    ne = logits.shape[1]
    lane = lax.broadcasted_iota(jnp.int32, logits.shape, 1).astype(F32)
    kcol = lax.broadcasted_iota(jnp.int32, (logits.shape[0], TOP_K), 1)
    vals = jnp.zeros((logits.shape[0], TOP_K), F32)
    idxs = jnp.zeros((logits.shape[0], TOP_K), F32)
    cur = logits
    for k in range(TOP_K):
        mx = jnp.max(cur, axis=1, keepdims=True)
        am = jnp.min(jnp.where(cur == mx, lane, float(ne)), axis=1, keepdims=True)
        vals = jnp.where(kcol == k, mx, vals)
        idxs = jnp.where(kcol == k, am, idxs)
        cur = jnp.where(lane == am, -jnp.inf, cur)
    e = jnp.exp(vals - vals[:, 0:1])
    gate_ref[...] = e / jnp.sum(e, axis=1, keepdims=True)
    idx_ref[...] = idxs.astype(jnp.int32)


def moe_router(x3, gain, layer, modexp, w_router, b_router):
    G, R, K = x3.shape
    E = w_router.shape[-1]
    bb = _pick(G, (64, 32, 16, 8, 4, 2, 1))
    tm = bb * R
    T = G * R
    kern = functools.partial(_router_kernel, bb=bb, rows=R)
    return pl.pallas_call(
        kern,
        grid=(G // bb,),
        in_specs=[pl.BlockSpec((bb, R, K), lambda g: (g, 0, 0)),
                  pl.BlockSpec((None, 1, K), lambda g: (layer, 0, 0)),
                  pl.BlockSpec((None, bb, 1, K), lambda g: (3, g, 0, 0)),
                  pl.BlockSpec((None, bb, 1, K), lambda g: (4, g, 0, 0)),
                  pl.BlockSpec((None, K, E), lambda g: (layer, 0, 0)),
                  pl.BlockSpec((None, 1, E), lambda g: (layer, 0, 0))],
        out_specs=[pl.BlockSpec((bb, R, K), lambda g: (g, 0, 0)),
                   pl.BlockSpec((tm, TOP_K), lambda g: (g, 0)),
                   pl.BlockSpec((tm, TOP_K), lambda g: (g, 0))],
        out_shape=[jax.ShapeDtypeStruct((G, R, K), F32),
                   jax.ShapeDtypeStruct((T, TOP_K), F32),
                   jax.ShapeDtypeStruct((T, TOP_K), jnp.int32)],
        scratch_shapes=[pltpu.VMEM((K, E), BF16), pltpu.VMEM((K, E), BF16)],
        compiler_params=_params(("arbitrary",)),
        name="moe_router",
    )(x3, gain.reshape(gain.shape[0], 1, K), modexp, modexp, w_router,
      b_router.reshape(b_router.shape[0], 1, E))


def _expert_kernel(bexp_ref, stok_ref, nblk_ref, h_hbm, sg_ref, wgu_ref, bgu_ref, wd_ref, bd_ref, o_ref,
                   xbuf, sem, wgu_bf, wd_bf, *, blk, d_ff):
    i = pl.program_id(0)
    n_used = nblk_ref[0]
    slot = i % 2

    def gather(block, s):
        def issue(r, c):
            tok = stok_ref[block * blk + r]
            pltpu.make_async_copy(h_hbm.at[pl.ds(tok, 1), :], xbuf.at[s, pl.ds(r, 1), :], sem.at[s]).start()
            return c
        lax.fori_loop(0, blk, issue, 0)

    @pl.when(i == 0)
    def _():
        gather(0, 0)

    @pl.when(i + 1 < n_used)
    def _():
        gather(i + 1, 1 - slot)

    @pl.when(i < n_used)
    def _():
        changed = jnp.logical_or(i == 0, bexp_ref[i] != bexp_ref[jnp.maximum(i - 1, 0)])

        @pl.when(changed)
        def _():
            wgu_bf[...] = wgu_ref[...].astype(BF16)
            wd_bf[...] = wd_ref[...].astype(BF16)

        pltpu.make_async_copy(h_hbm.at[pl.ds(0, blk), :], xbuf.at[slot], sem.at[slot]).wait()
        x = xbuf[slot].astype(BF16)
        gu = _dot(x, wgu_bf[...]) + bgu_ref[...]
        gate = jnp.minimum(gu[:, :d_ff], SWIGLU_LIMIT)
        up = jnp.clip(gu[:, d_ff:], -SWIGLU_LIMIT, SWIGLU_LIMIT)
        act = (up + 1.0) * gate * _sigmoid(SWIGLU_ALPHA * gate)
        y = _dot(act.astype(BF16), wd_bf[...]) + bd_ref[...]
        o_ref[...] = y * sg_ref[...]

    @pl.when(i >= n_used)
    def _():
        o_ref[...] = jnp.zeros(o_ref.shape, F32)


def moe_experts(h2, block_expert, slot_tok, n_used, slot_gate, w_gu, b_gu, w_down, b_down, *, layer, blk):
    T, D = h2.shape
    n_blocks = block_expert.shape[0]
    E, _, F2 = w_gu.shape[1:]
    d_ff = F2 // 2
    grid_spec = pltpu.PrefetchScalarGridSpec(
        num_scalar_prefetch=3,
        grid=(n_blocks,),
        in_specs=[pl.BlockSpec(memory_space=pl.ANY),
                  pl.BlockSpec((blk, 1), lambda i, be, st, nu: (i, 0)),
                  pl.BlockSpec((None, None, D, F2), lambda i, be, st, nu: (layer, be[i], 0, 0)),
                  pl.BlockSpec((None, None, 1, F2), lambda i, be, st, nu: (layer, be[i], 0, 0)),
                  pl.BlockSpec((None, None, d_ff, D), lambda i, be, st, nu: (layer, be[i], 0, 0)),
                  pl.BlockSpec((None, None, 1, D), lambda i, be, st, nu: (layer, be[i], 0, 0))],
        out_specs=pl.BlockSpec((blk, D), lambda i, be, st, nu: (i, 0)),
        scratch_shapes=[pltpu.VMEM((2, blk, D), F32), pltpu.SemaphoreType.DMA((2,)),
                        pltpu.VMEM((D, F2), BF16), pltpu.VMEM((d_ff, D), BF16)],
    )
    return pl.pallas_call(
        functools.partial(_expert_kernel, blk=blk, d_ff=d_ff),
        grid_spec=grid_spec,
        out_shape=jax.ShapeDtypeStruct((n_blocks * blk, D), F32),
        compiler_params=_params(("arbitrary",)),
        name="moe_experts",
    )(block_expert, slot_tok, n_used, h2, slot_gate, w_gu,
      b_gu.reshape(b_gu.shape[0], E, 1, F2), w_down, b_down.reshape(b_down.shape[0], E, 1, D))


def _combine_kernel(inv_ref, yb_hbm, res_ref, gate_ref, o_ref, buf, sem, *, tm, bb, rows):
    i = pl.program_id(0)
    n = pl.num_programs(0)
    slot = i % 2

    def gather(tile, s):
        def issue(r, c):
            for k in range(TOP_K):
                src = inv_ref[(tile * tm + r) * TOP_K + k]
                pltpu.make_async_copy(yb_hbm.at[pl.ds(src, 1), :], buf.at[s, k, pl.ds(r, 1), :], sem.at[s]).start()
            return c
        lax.fori_loop(0, tm, issue, 0)

    @pl.when(i == 0)
    def _():
        gather(0, 0)

    @pl.when(i + 1 < n)
    def _():
        gather(i + 1, 1 - slot)

    for k in range(TOP_K):
        pltpu.make_async_copy(yb_hbm.at[pl.ds(0, tm), :], buf.at[slot, k], sem.at[slot]).wait()
    y = buf[slot, 0] + buf[slot, 1] + buf[slot, 2] + buf[slot, 3]
    o_ref[...] = res_ref[...] + gate_ref[...] * y.reshape(bb, rows, y.shape[-1])


def moe_combine(yb, inv_slot, xres3, modexp):
    G, R, D = xres3.shape
    bb = _pick(G, (16, 8, 4, 2, 1))
    tm = bb * R
    grid_spec = pltpu.PrefetchScalarGridSpec(
        num_scalar_prefetch=1,
        grid=(G // bb,),
        in_specs=[pl.BlockSpec(memory_space=pl.ANY),
                  pl.BlockSpec((bb, R, D), lambda g, inv: (g, 0, 0)),
                  pl.BlockSpec((None, bb, 1, D), lambda g, inv: (5, g, 0, 0))],
        out_specs=pl.BlockSpec((bb, R, D), lambda g, inv: (g, 0, 0)),
        scratch_shapes=[pltpu.VMEM((2, TOP_K, tm, D), F32), pltpu.SemaphoreType.DMA((2,))],
    )
    return pl.pallas_call(
        functools.partial(_combine_kernel, tm=tm, bb=bb, rows=R),
        grid_spec=grid_spec,
        out_shape=jax.ShapeDtypeStruct((G, R, D), F32),
        compiler_params=_params(("arbitrary",)),
        name="moe_combine",
    )(inv_slot, yb, xres3, modexp)


def moe_layer(x3, modexp, layer, norm_ffn, w_router, b_router, w_gu, b_gu, w_down, b_down, blk):
    G, R, D = x3.shape
    T = G * R
    E = w_router.shape[-1]
    h3, gates, idx = moe_router(x3, norm_ffn, layer, modexp, w_router, b_router)
    n_assign = T * TOP_K
    flat_e = idx.reshape(-1)
    order = jnp.argsort(flat_e).astype(jnp.int32)
    e_sorted = flat_e[order]
    counts = jnp.bincount(flat_e, length=E).astype(jnp.int32)
    padded = (counts + blk - 1) // blk * blk
    pad_end = jnp.cumsum(padded)
    pad_start = pad_end - padded
    start = jnp.cumsum(counts) - counts
    dest = pad_start[e_sorted] + jnp.arange(n_assign, dtype=jnp.int32) - start[e_sorted]
    n_blocks = -(-n_assign // blk) + E
    slot_tok = jnp.zeros((n_blocks * blk,), jnp.int32).at[dest].set(order // TOP_K)
    slot_gate = jnp.zeros((n_blocks * blk,), F32).at[dest].set(gates.reshape(-1)[order])
    block_expert = jnp.minimum(
        jnp.searchsorted(pad_end, jnp.arange(n_blocks, dtype=jnp.int32) * blk, side="right"), E - 1).astype(jnp.int32)
    n_used = (pad_end[-1:] // blk).astype(jnp.int32)
    inv_slot = jnp.zeros((n_assign,), jnp.int32).at[order].set(dest)
    yb = moe_experts(h3.reshape(T, D), block_expert, slot_tok, n_used, slot_gate.reshape(-1, 1),
                     w_gu, b_gu, w_down, b_down, layer=layer, blk=blk)
    return moe_combine(yb, inv_slot, x3, modexp)


def kernel(x_prompt, x_sample, cache_a_k, cache_a_v, state_b_re, state_b_im, cache_c_k, cache_c_v, page_table, c_prompt, c_sample, w_ada, b_ada, norm_mix, norm_ffn, a_w_in, a_q_norm, a_k_norm, a_lambda, a_sub_norm, a_w_out, b_log_dt, b_a_re, b_a_im, b_b_re, b_b_im, b_c_re, b_c_im, b_d, b_w_glu, b_b_glu, c_w_in, c_w_out, moe_w_router, moe_b_router, moe_w_gu, moe_b_gu, moe_w_down, moe_b_down):
    batch, seq, d = x_prompt.shape
    nseq, dec_seq, _ = x_sample.shape
    assert dec_seq == GROUP and seq % GROUP == 0
    depth = w_ada.shape[0]
    n_pages, page = page_table.shape[1], cache_a_k.shape[2]
    past = n_pages * page
    tp, ts = batch * seq, nseq * dec_seq
    T = tp + ts
    G = T // GROUP
    gp = tp // GROUP
    h_a, d_ha = cache_a_k.shape[3], cache_a_k.shape[5]
    h_c, d_hc = cache_c_k.shape[3], cache_c_k.shape[4]
    ssm_g, ssm_p = b_a_re.shape[1], b_a_re.shape[2]
    n_state = ssm_g * ssm_p

    x3 = jnp.concatenate([x_prompt.reshape(gp, GROUP, d), x_sample], axis=0)

    nc = batch + nseq
    ncp = -(-nc // GROUP) * GROUP
    c_all = jnp.concatenate([c_prompt, c_sample, jnp.zeros((ncp - nc, d), F32)], axis=0).reshape(1, ncp, d)
    modexps = []
    for i in range(depth):
        m = fused_matmul(c_all, w_ada, b_ada, layer=i, silu_in=True, passes=3, name="ada")[0]
        m = m.reshape(ncp, 6, d).transpose(1, 0, 2)
        mexp = jnp.concatenate([jnp.repeat(m[:, :batch], seq // GROUP, axis=1), m[:, batch:nc]], axis=1)
        modexps.append(mexp.reshape(6, G, 1, d))

    pos = jnp.concatenate([jnp.tile(jnp.arange(seq, dtype=jnp.int32), batch),
                           jnp.tile(past + jnp.arange(dec_seq, dtype=jnp.int32), nseq)])
    freqs = ROPE_THETA ** (-jnp.arange(0, d_ha, 2, dtype=F32) / d_ha)
    ang = pos.astype(F32)[:, None] * freqs[None, :]
    cos, sin = jnp.cos(ang), jnp.sin(ang)
    reps = LANES // d_ha
    cos_t = jnp.tile(jnp.concatenate([cos, cos], axis=1), (1, reps))
    sin_t = jnp.tile(jnp.concatenate([-sin, sin], axis=1), (1, reps))

    zeros_state = jnp.zeros((batch, n_state), F32)
    moe_blk = 256 if T * TOP_K >= 32768 else 128
    new_a, new_b, new_c = [], [], []
    for i in range(depth):
        kind, j = i % N_MIXERS, i // N_MIXERS
        mexp = modexps[i]
        if kind == 0:
            lam_init = 0.8 - 0.6 * math.exp(-0.3 * i)
            nqk = h_a * 2 * d_ha
            qkv3 = fused_matmul(x3, a_w_in, layer=j, norm=(norm_mix, i, mexp, 0, 1), name="a_qkv")
            qkv2 = qkv3.reshape(T, 3 * nqk)
            gains = jnp.stack([jnp.tile(a_q_norm[j], reps) * (d_ha ** -0.5), jnp.tile(a_k_norm[j], reps)])
            qk2 = qknorm_rope(qkv2, gains.reshape(2, 1, LANES), cos_t, sin_t, 2 * nqk, d_ha)
            o2 = diff_attention_prompt(qk2, qkv2, a_sub_norm, a_lambda, batch=batch, seq=seq, n_heads=h_a,
                                       d_head=d_ha, lam_init=lam_init, layer=j)
            o2 = diff_attention_sample(o2, qk2, qkv2, cache_a_k, cache_a_v, page_table, a_sub_norm, a_lambda,
                                       n_prompt_rows=tp, n_heads=h_a, d_head=d_ha, nq=dec_seq,
                                       lam_init=lam_init, layer=j)
            x3 = fused_matmul(o2.reshape(G, GROUP, nqk), a_w_out, layer=j, resid=(x3, mexp, 2), name="a_out")
            k2, v2 = qk2[:, nqk:], qkv2[:, 2 * nqk:]
            new_a.append((k2[:tp].reshape(batch, seq, h_a, 2, d_ha), v2[:tp].reshape(batch, seq, h_a, 2 * d_ha),
                          k2[tp:].reshape(nseq, dec_seq, h_a, 2, d_ha), v2[tp:].reshape(nseq, dec_seq, h_a, 2 * d_ha)))
        elif kind == 1:
            u3 = mod_norm(x3, norm_mix, i, mexp, 0, 1)
            wb, ar, ai, wc = s5_tables(b_log_dt[j], b_a_re[j], b_a_im[j], b_b_re[j], b_b_im[j],
                                       b_c_re[j], b_c_im[j], ssm_g // GROUP)
            d_skip = b_d[j].reshape(1, d)
            y_p, hpr, hpi = s5_scan(u3[:gp].reshape(batch, seq, d), zeros_state, zeros_state, wb, ar, ai, wc,
                                    d_skip, nb=batch, tt=_pick(seq, (64, 32, 16, 8)), name="s5_prompt")
            y_s, hsr, hsi = s5_scan(u3[gp:], state_b_re[j].reshape(nseq, n_state), state_b_im[j].reshape(nseq, n_state),
                                    wb, ar, ai, wc, d_skip, nb=_pick(nseq, (32, 16, 8)), tt=dec_seq, name="s5_sample")
            y3 = jnp.concatenate([y_p.reshape(gp, GROUP, d), y_s], axis=0)
            x3 = fused_matmul(y3, b_w_glu, b_b_glu, layer=j, glu=True, resid=(x3, mexp, 2), name="b_glu")
            st = lambda a, n: a.reshape(n, ssm_g, ssm_p)
            new_b.append((st(hpr, batch), st(hpi, batch), st(hsr, nseq), st(hsi, nseq)))
        else:
            dm = h_c * d_hc
            qkv3 = fused_matmul(x3, c_w_in, layer=j, norm=(norm_mix, i, mexp, 0, 1), name="c_qkv")
            qkv2 = qkv3.reshape(T, 3 * dm)
            o2 = sb_attention_prompt(qkv2, batch=batch, seq=seq, n_heads=h_c, d_head=d_hc)
            o2 = sb_attention_sample(o2, qkv2, cache_c_k, cache_c_v, page_table, n_prompt_rows=tp,
                                     n_heads=h_c, d_head=d_hc, nq=dec_seq, layer=j)
            x3 = fused_matmul(o2.reshape(G, GROUP, dm), c_w_out, layer=j, resid=(x3, mexp, 2), name="c_out")
            k2, v2 = qkv2[:, dm:2 * dm], qkv2[:, 2 * dm:]
            new_c.append((k2[:tp].reshape(batch, seq, h_c, d_hc), v2[:tp].reshape(batch, seq, h_c, d_hc),
                          k2[tp:].reshape(nseq, dec_seq, h_c, d_hc), v2[tp:].reshape(nseq, dec_seq, h_c, d_hc)))
        x3 = moe_layer(x3, mexp, i, norm_ffn, moe_w_router, moe_b_router, moe_w_gu, moe_b_gu,
                       moe_w_down, moe_b_down, moe_blk)

    outs = [x3[:gp].reshape(batch, seq, d), x3[gp:]]
    for group in (new_a, new_b, new_c):
        for k in range(4):
            outs.append(jnp.stack([entry[k] for entry in group]))
    return tuple(outs)
```

```python
import functools
import math

import jax
import jax.numpy as jnp
from jax import lax
from jax.experimental import pallas as pl
from jax.experimental.pallas import tpu as pltpu

F32 = jnp.float32
BF16 = jnp.bfloat16

EPS = 1e-6
NEG_INF = -1e30
ROPE_THETA = 10000.0
TOP_K = 4
SWIGLU_LIMIT = 7.0
SWIGLU_ALPHA = 1.702
N_MIXERS = 3
GROUP = 8
LANES = 128
VMEM_LIMIT = 56 * 1024 * 1024


def _params(sem, vmem=VMEM_LIMIT):
    return pltpu.CompilerParams(dimension_semantics=sem, vmem_limit_bytes=vmem)


def _pick(n, cands):
    for c in cands:
        if n % c == 0:
            return c
    raise ValueError(f"no tile for {n} in {cands}")


def _split2(x):
    hi = x.astype(BF16)
    lo = (x - hi.astype(F32)).astype(BF16)
    return hi, lo


def _sigmoid(x):
    return 1.0 / (1.0 + jnp.exp(-x))


def _dot(a, b):
    return jnp.dot(a, b, preferred_element_type=F32)


def _dot_nt(a, b):
    return lax.dot_general(a, b, (((1,), (1,)), ((), ())), preferred_element_type=F32)


def _mm_kernel(*refs, norm, silu_in, glu, has_bias, resid, passes, bb, rows):
    it = iter(refs)
    x_ref = next(it)
    if norm:
        g_ref, sh_ref, sc_ref = next(it), next(it), next(it)
    w_refs = [next(it)] + ([next(it)] if glu else [])
    b_refs = ([next(it)] + ([next(it)] if glu else [])) if has_bias else []
    if resid:
        res_ref, gate_ref = next(it), next(it)
    o_ref = next(it)
    whi = [next(it) for _ in w_refs]
    wlo = [next(it) for _ in w_refs] if passes == 3 else []

    @pl.when(pl.program_id(1) == 0)
    def _():
        for i, w_ref in enumerate(w_refs):
            w = w_ref[...]
            hi = w.astype(BF16)
            whi[i][...] = hi
            if passes == 3:
                wlo[i][...] = (w - hi.astype(F32)).astype(BF16)

    x = x_ref[...]
    if silu_in:
        x = x * _sigmoid(x)
    if norm:
        ms = jnp.mean(x * x, axis=-1, keepdims=True)
        x = x * lax.rsqrt(ms + EPS) * g_ref[...]
        x = x * (1.0 + sc_ref[...]) + sh_ref[...]
    x2 = x.reshape(bb * rows, x.shape[-1])
    xh = x2.astype(BF16)
    xl = (x2 - xh.astype(F32)).astype(BF16) if passes == 3 else None

    def mm(i):
        acc = _dot(xh, whi[i][...])
        if passes == 3:
            acc = acc + _dot(xl, whi[i][...]) + _dot(xh, wlo[i][...])
        if has_bias:
            acc = acc + b_refs[i][...]
        return acc

    z = mm(0)
    if glu:
        z = z * _sigmoid(mm(1))
    z3 = z.reshape(bb, rows, z.shape[-1])
    if resid:
        z3 = res_ref[...] + gate_ref[...] * z3
    o_ref[...] = z3


def fused_matmul(x3, w, b=None, *, layer=0, norm=None, silu_in=False, glu=False,
                 resid=None, passes=1, tn=None, name="mm"):
    G, R, K = x3.shape
    N = w.shape[-1]
    n_out = N // 2 if glu else N
    bb = _pick(G, (64, 32, 16, 8, 4, 2, 1))
    tn = tn or min(n_out, 1024)
    assert n_out % tn == 0
    half = n_out // tn

    in_specs = [pl.BlockSpec((bb, R, K), lambda n, g: (g, 0, 0))]
    args = [x3]
    if norm is not None:
        gain, gl, modexp, i_sh, i_sc = norm
        gain3 = gain.reshape(gain.shape[0], 1, K)
        in_specs += [pl.BlockSpec((None, 1, K), lambda n, g: (gl, 0, 0)),
                     pl.BlockSpec((None, bb, 1, K), lambda n, g: (i_sh, g, 0, 0)),
                     pl.BlockSpec((None, bb, 1, K), lambda n, g: (i_sc, g, 0, 0))]
        args += [gain3, modexp, modexp]
    in_specs.append(pl.BlockSpec((None, K, tn), lambda n, g: (layer, 0, n)))
    args.append(w)
    if glu:
        in_specs.append(pl.BlockSpec((None, K, tn), lambda n, g: (layer, 0, n + half)))
        args.append(w)
    if b is not None:
        b3 = b.reshape(b.shape[0], 1, N)
        in_specs.append(pl.BlockSpec((None, 1, tn), lambda n, g: (layer, 0, n)))
        args.append(b3)
        if glu:
            in_specs.append(pl.BlockSpec((None, 1, tn), lambda n, g: (layer, 0, n + half)))
            args.append(b3)
    if resid is not None:
        xres, modexp_r, i_g = resid
        in_specs += [pl.BlockSpec((bb, R, tn), lambda n, g: (g, 0, n)),
                     pl.BlockSpec((None, bb, 1, tn), lambda n, g: (i_g, g, 0, n))]
        args += [xres, modexp_r]
    nw = 2 if glu else 1
    scratch = [pltpu.VMEM((K, tn), BF16) for _ in range(nw * (2 if passes == 3 else 1))]
    kern = functools.partial(_mm_kernel, norm=norm is not None, silu_in=silu_in, glu=glu,
                             has_bias=b is not None, resid=resid is not None, passes=passes,
                             bb=bb, rows=R)
    return pl.pallas_call(
        kern,
        grid=(n_out // tn, G // bb),
        in_specs=in_specs,
        out_specs=pl.BlockSpec((bb, R, tn), lambda n, g: (g, 0, n)),
        out_shape=jax.ShapeDtypeStruct((G, R, n_out), F32),
        scratch_shapes=scratch,
        compiler_params=_params(("arbitrary", "arbitrary")),
        name=name,
    )(*args)


def _mod_norm_kernel(x_ref, g_ref, sh_ref, sc_ref, o_ref):
    x = x_ref[...]
    ms = jnp.mean(x * x, axis=-1, keepdims=True)
    x = x * lax.rsqrt(ms + EPS) * g_ref[...]
    o_ref[...] = x * (1.0 + sc_ref[...]) + sh_ref[...]


def mod_norm(x3, gain, layer, modexp, i_sh, i_sc):
    G, R, K = x3.shape
    bb = _pick(G, (64, 32, 16, 8, 4, 2, 1))
    return pl.pallas_call(
        _mod_norm_kernel,
        grid=(G // bb,),
        in_specs=[pl.BlockSpec((bb, R, K), lambda g: (g, 0, 0)),
                  pl.BlockSpec((None, 1, K), lambda g: (layer, 0, 0)),
                  pl.BlockSpec((None, bb, 1, K), lambda g: (i_sh, g, 0, 0)),
                  pl.BlockSpec((None, bb, 1, K), lambda g: (i_sc, g, 0, 0))],
        out_specs=pl.BlockSpec((bb, R, K), lambda g: (g, 0, 0)),
        out_shape=jax.ShapeDtypeStruct((G, R, K), F32),
        compiler_params=_params(("parallel",)),
        name="mod_norm",
    )(x3, gain.reshape(gain.shape[0], 1, K), modexp, modexp)


def _qknorm_rope_kernel(x_ref, g_ref, cos_ref, sin_ref, o_ref, *, d_head):
    x = x_ref[...]
    r = lax.broadcasted_iota(jnp.int32, (LANES, LANES), 0) // d_head
    c = lax.broadcasted_iota(jnp.int32, (LANES, LANES), 1) // d_head
    seg = jnp.where(r == c, 1.0, 0.0).astype(BF16)
    hi, lo = _split2(x * x)
    ss = _dot(hi, seg) + _dot(lo, seg)
    y = x * lax.rsqrt(ss * (1.0 / d_head) + EPS) * g_ref[...]
    lane = lax.broadcasted_iota(jnp.int32, x.shape, 1)
    half = d_head // 2
    rot = jnp.where(lane % d_head < half, pltpu.roll(y, LANES - half, axis=1), pltpu.roll(y, half, axis=1))
    o_ref[...] = y * cos_ref[...] + rot * sin_ref[...]


def qknorm_rope(qkv2, gains, cos_t, sin_t, n_qk_cols, d_head):
    T = qkv2.shape[0]
    tm = _pick(T, (1024, 512, 256, 128, 64, 32, 16, 8))
    ncb = n_qk_cols // LANES
    per = ncb // 2
    return pl.pallas_call(
        functools.partial(_qknorm_rope_kernel, d_head=d_head),
        grid=(T // tm, ncb),
        in_specs=[pl.BlockSpec((tm, LANES), lambda i, j: (i, j)),
                  pl.BlockSpec((None, 1, LANES), lambda i, j: (j // per, 0, 0)),
                  pl.BlockSpec((tm, LANES), lambda i, j: (i, 0)),
                  pl.BlockSpec((tm, LANES), lambda i, j: (i, 0))],
        out_specs=pl.BlockSpec((tm, LANES), lambda i, j: (i, j)),
        out_shape=jax.ShapeDtypeStruct((T, n_qk_cols), F32),
        compiler_params=_params(("parallel", "parallel")),
        name="qknorm_rope",
    )(qkv2, gains, cos_t, sin_t)


def _lambda_value(alam_ref, lam_init):
    a = alam_ref[...]
    s1 = jnp.sum(a[0:1] * a[1:2], axis=1, keepdims=True)
    s2 = jnp.sum(a[2:3] * a[3:4], axis=1, keepdims=True)
    return jnp.exp(s1) - jnp.exp(s2) + lam_init


def _softmax_update(s, v_bf, m_ref, l_ref, acc_ref, reps_s, reps_acc):
    m_prev = m_ref[...]
    m_new = jnp.maximum(m_prev, jnp.max(s, axis=1, keepdims=True))
    alpha = jnp.exp(m_prev - m_new)
    m_b = m_new if reps_s == 1 else jnp.concatenate([m_new] * reps_s, axis=1)
    p = jnp.exp(s - m_b)
    l_ref[...] = alpha * l_ref[...] + jnp.sum(p, axis=1, keepdims=True)
    a_b = alpha if reps_acc == 1 else jnp.concatenate([alpha] * reps_acc, axis=1)
    acc_ref[...] = a_b * acc_ref[...] + _dot(p.astype(BF16), v_bf)
    m_ref[...] = m_new


def _diff_prompt_kernel(q_ref, k_ref, v_ref, g_ref, alam_ref, o_ref,
                        kbf, vbf, m0, l0, a0, m1, l1, a1, *, tq, d_head, lam_init):
    qi = pl.program_id(2)

    @pl.when(qi == 0)
    def _():
        kbf[...] = k_ref[...].astype(BF16)
        vbf[...] = v_ref[...].astype(BF16)

    q = q_ref[...]
    lane = lax.broadcasted_iota(jnp.int32, q.shape, 1)
    qc = (jnp.where(lane < d_head, q, 0.0).astype(BF16), jnp.where(lane >= d_head, q, 0.0).astype(BF16))
    stats = ((m0, l0, a0), (m1, l1, a1))
    for m, l, a in stats:
        m[...] = jnp.full(m.shape, NEG_INF, F32)
        l[...] = jnp.zeros(l.shape, F32)
        a[...] = jnp.zeros(a.shape, F32)
    reps = tq // LANES

    def chunk(kj, masked):
        off = pl.multiple_of(kj * tq, tq)
        kb = kbf[pl.ds(off, tq), :]
        vb = vbf[pl.ds(off, tq), :]
        for c in range(2):
            s = _dot_nt(qc[c], kb)
            if masked:
                row = lax.broadcasted_iota(jnp.int32, s.shape, 0)
                col = lax.broadcasted_iota(jnp.int32, s.shape, 1)
                s = jnp.where(col <= row, s, NEG_INF)
            _softmax_update(s, vb, *stats[c], reps, 1)

    def body(kj, carry):
        chunk(kj, False)
        return carry

    lax.fori_loop(0, qi, body, 0)
    chunk(qi, True)

    lam = _lambda_value(alam_ref, lam_init)
    o = a0[...] / l0[...] - lam * (a1[...] / l1[...])
    ms = jnp.mean(o * o, axis=1, keepdims=True)
    o_ref[...] = o * lax.rsqrt(ms + EPS) * g_ref[...] * (1.0 - lam_init)


def diff_attention_prompt(qk2, qkv2, sub_gain, alam, *, batch, seq, n_heads, d_head, lam_init, layer):
    T = qk2.shape[0]
    dv = 2 * d_head
    assert dv == LANES
    tq = _pick(seq, (512, 256, 128))
    nq = seq // tq
    kcol = n_heads
    vcol = 2 * n_heads
    kern = functools.partial(_diff_prompt_kernel, tq=tq, d_head=d_head, lam_init=lam_init)
    return pl.pallas_call(
        kern,
        grid=(batch, n_heads, nq),
        in_specs=[pl.BlockSpec((tq, LANES), lambda b, h, i: (b * nq + i, h)),
                  pl.BlockSpec((seq, LANES), lambda b, h, i: (b, kcol + h)),
                  pl.BlockSpec((seq, LANES), lambda b, h, i: (b, vcol + h)),
                  pl.BlockSpec((None, 1, LANES), lambda b, h, i: (layer, 0, 0)),
                  pl.BlockSpec((None, 4, d_head), lambda b, h, i: (layer, 0, 0))],
        out_specs=pl.BlockSpec((tq, LANES), lambda b, h, i: (b * nq + i, h)),
        out_shape=jax.ShapeDtypeStruct((T, n_heads * dv), F32),
        scratch_shapes=[pltpu.VMEM((seq, LANES), BF16), pltpu.VMEM((seq, LANES), BF16)]
        + [pltpu.VMEM((tq, LANES), F32) for _ in range(6)],
        compiler_params=_params(("arbitrary", "arbitrary", "arbitrary")),
        name="diff_attn_prompt",
    )(qk2, qk2, qkv2, sub_gain.reshape(sub_gain.shape[0], 1, dv), alam)


def _diff_sample_kernel(pt_ref, q_ref, kn_ref, vn_ref, *rest, n_steps, pps, n_heads, d_head, nq, lam_init):
    del pt_ref
    kp_refs, vp_refs = rest[:pps], rest[pps:2 * pps]
    g_ref, alam_ref, oin_ref, o_ref, qbd, m, l, acc = rest[2 * pps:]
    del oin_ref
    p = pl.program_id(1)
    dm = n_heads * 2 * d_head
    dv = 2 * d_head
    rph = 2 * nq
    nrow = n_heads * rph

    @pl.when(p == 0)
    def _():
        row = lax.broadcasted_iota(jnp.int32, (nrow, dm), 0)
        col = lax.broadcasted_iota(jnp.int32, (nrow, dm), 1)
        qt = jnp.concatenate([q_ref[...]] * (2 * n_heads), axis=0)
        qbd[...] = jnp.where(col // d_head == row // nq, qt, 0.0).astype(BF16)
        m[...] = jnp.full(m.shape, NEG_INF, F32)
        l[...] = jnp.zeros(l.shape, F32)
        acc[...] = jnp.zeros(acc.shape, F32)

    def update(s, v_of_head):
        m_prev = m[...]
        m_new = jnp.maximum(m_prev, jnp.max(s, axis=1, keepdims=True))
        alpha = jnp.exp(m_prev - m_new)
        pr = jnp.exp(s - m_new)
        l[...] = alpha * l[...] + jnp.sum(pr, axis=1, keepdims=True)
        pb = pr.astype(BF16)
        pv = jnp.concatenate([_dot(pb[h * rph:(h + 1) * rph], v_of_head(h)) for h in range(n_heads)], axis=0)
        acc[...] = alpha * acc[...] + pv
        m[...] = m_new

    for i in range(pps):
        s = _dot(qbd[...], kp_refs[i][...].astype(BF16))
        update(s, lambda h, i=i: vp_refs[i][pl.ds(h, LANES, stride=n_heads), :].astype(BF16))

    @pl.when(p == n_steps - 1)
    def _():
        kn = jnp.concatenate([kn_ref[...], jnp.zeros((LANES - nq, dm), F32)], axis=0).astype(BF16)
        r = lax.broadcasted_iota(jnp.int32, (nrow, LANES), 0)
        j = lax.broadcasted_iota(jnp.int32, (nrow, LANES), 1)
        s = jnp.where(j <= r % nq, _dot_nt(qbd[...], kn), NEG_INF)
        vpad = jnp.zeros((LANES - nq, dv), F32)
        update(s, lambda h: jnp.concatenate([vn_ref[:, h * dv:(h + 1) * dv], vpad], axis=0).astype(BF16))
        lam = _lambda_value(alam_ref, lam_init)
        on = acc[...] / l[...]
        for h in range(n_heads):
            oh = on[h * rph:h * rph + nq] - lam * on[h * rph + nq:(h + 1) * rph]
            ms = jnp.mean(oh * oh, axis=1, keepdims=True)
            o_ref[:, h * dv:(h + 1) * dv] = oh * lax.rsqrt(ms + EPS) * g_ref[...] * (1.0 - lam_init)


def diff_attention_sample(o_prompt, qk2, qkv2, cache_k, cache_v, page_table, sub_gain, alam, *,
                          n_prompt_rows, n_heads, d_head, nq, lam_init, layer):
    T, dm = o_prompt.shape
    nseq, n_pages = page_table.shape
    n_layers, n_pool, page = cache_k.shape[:3]
    dv = 2 * d_head
    assert page == LANES and 2 * n_heads * nq == LANES and nq == GROUP and dv == LANES
    ck = jnp.transpose(cache_k, (0, 1, 3, 4, 5, 2)).reshape(n_layers, n_pool, dm, page)
    cv = cache_v.reshape(n_layers, n_pool, page * n_heads, dv)
    r0 = n_prompt_rows // nq
    pps = _pick(n_pages, (4, 2, 1))
    n_steps = n_pages // pps

    def page_map(i):
        return lambda b, p, pt: (layer, pt[b * n_pages + p * pps + i], 0, 0)

    in_specs = [pl.BlockSpec((nq, dm), lambda b, p, pt: (r0 + b, 0)),
                pl.BlockSpec((nq, dm), lambda b, p, pt: (r0 + b, 1)),
                pl.BlockSpec((nq, dm), lambda b, p, pt: (r0 + b, 2))]
    in_specs += [pl.BlockSpec((None, None, dm, page), page_map(i)) for i in range(pps)]
    in_specs += [pl.BlockSpec((None, None, page * n_heads, dv), page_map(i)) for i in range(pps)]
    in_specs += [pl.BlockSpec((None, 1, dv), lambda b, p, pt: (layer, 0, 0)),
                 pl.BlockSpec((None, 4, d_head), lambda b, p, pt: (layer, 0, 0)),
                 pl.BlockSpec(memory_space=pl.ANY)]
    grid_spec = pltpu.PrefetchScalarGridSpec(
        num_scalar_prefetch=1,
        grid=(nseq, n_steps),
        in_specs=in_specs,
        out_specs=pl.BlockSpec((nq, dm), lambda b, p, pt: (r0 + b, 0)),
        scratch_shapes=[pltpu.VMEM((LANES, dm), BF16), pltpu.VMEM((LANES, LANES), F32),
                        pltpu.VMEM((LANES, LANES), F32), pltpu.VMEM((LANES, dv), F32)],
    )
    kern = functools.partial(_diff_sample_kernel, n_steps=n_steps, pps=pps, n_heads=n_heads, d_head=d_head,
                             nq=nq, lam_init=lam_init)
    return pl.pallas_call(
        kern,
        grid_spec=grid_spec,
        out_shape=jax.ShapeDtypeStruct((T, dm), F32),
        input_output_aliases={6 + 2 * pps: 0},
        compiler_params=_params(("arbitrary", "arbitrary")),
        name="diff_attn_sample",
    )(page_table.reshape(-1), qk2, qk2, qkv2, *([ck] * pps), *([cv] * pps),
      sub_gain.reshape(sub_gain.shape[0], 1, dv), alam, o_prompt)


def _later_matrix(n):
    j = lax.broadcasted_iota(jnp.int32, (n, n), 0)
    s = lax.broadcasted_iota(jnp.int32, (n, n), 1)
    return jnp.where(j > s, 1.0, 0.0).astype(BF16)


def _stick_block(zn, log_stay, carry, later, mask):
    reps = zn.shape[1] // LANES
    hi, lo = _split2(log_stay)
    c_b = carry if reps == 1 else jnp.concatenate([carry] * reps, axis=1)
    log_later = _dot(hi, later) + _dot(lo, later) + c_b
    w = jnp.exp(log_stay - zn + log_later)
    if mask is not None:
        w = jnp.where(mask, w, 0.0)
    return w, carry + jnp.sum(log_stay, axis=1, keepdims=True)


def _log_stay(zn, mask):
    ls = jnp.minimum(zn, 0.0) - jnp.log(1.0 + jnp.exp(-jnp.abs(zn)))
    return ls if mask is None else jnp.where(mask, ls, 0.0)


def _sb_prompt_kernel(q_ref, k_ref, v_ref, o_ref, kbf, vbf, carry0, acc0, carry1, acc1, *, tq, cb, d_head, scale):
    qi = pl.program_id(2)

    @pl.when(qi == 0)
    def _():
        kbf[...] = k_ref[...].astype(BF16)
        vbf[...] = v_ref[...].astype(BF16)

    q = q_ref[...] * (-scale)
    lane = lax.broadcasted_iota(jnp.int32, q.shape, 1)
    first = lane < d_head
    qh = (jnp.where(first, q, 0.0).astype(BF16), jnp.where(first, 0.0, q).astype(BF16))
    later = _later_matrix(cb)
    state = ((carry0, acc0), (carry1, acc1))
    for c_ref, a_ref in state:
        c_ref[...] = jnp.zeros(c_ref.shape, F32)
        a_ref[...] = jnp.zeros(a_ref.shape, F32)

    def chunk(kj, masked):
        off = pl.multiple_of(kj * tq, tq)
        kb = kbf[pl.ds(off, tq), :]
        vb = vbf[pl.ds(off, tq), :]
        mask = None
        if masked:
            row = lax.broadcasted_iota(jnp.int32, (tq, tq), 0)
            col = lax.broadcasted_iota(jnp.int32, (tq, tq), 1)
            mask = col < row
        for h in range(2):
            c_ref, a_ref = state[h]
            zn = _dot_nt(qh[h], kb)
            ls = _log_stay(zn, mask)
            c = c_ref[...]
            acc = a_ref[...]
            for b in reversed(range(tq // cb)):
                sl = slice(b * cb, (b + 1) * cb)
                w, c = _stick_block(zn[:, sl], ls[:, sl], c, later, None if mask is None else mask[:, sl])
                acc = acc + _dot(w.astype(BF16), vb[sl, :])
            c_ref[...] = c
            a_ref[...] = acc

    chunk(qi, True)

    def body(t, c):
        chunk(qi - 1 - t, False)
        return c

    lax.fori_loop(0, qi, body, 0)
    o_ref[...] = jnp.where(first, acc0[...], acc1[...])


def sb_attention_prompt(qkv2, *, batch, seq, n_heads, d_head):
    T = qkv2.shape[0]
    dm = n_heads * d_head
    tq = _pick(seq, (512, 256, 128))
    cb = min(256, tq)
    nq = seq // tq
    npair = dm // LANES
    kern = functools.partial(_sb_prompt_kernel, tq=tq, cb=cb, d_head=d_head, scale=d_head ** -0.5)
    return pl.pallas_call(
        kern,
        grid=(batch, npair, nq),
        in_specs=[pl.BlockSpec((tq, LANES), lambda b, h, i: (b * nq + i, h)),
                  pl.BlockSpec((seq, LANES), lambda b, h, i: (b, npair + h)),
                  pl.BlockSpec((seq, LANES), lambda b, h, i: (b, 2 * npair + h))],
        out_specs=pl.BlockSpec((tq, LANES), lambda b, h, i: (b * nq + i, h)),
        out_shape=jax.ShapeDtypeStruct((T, dm), F32),
        scratch_shapes=[pltpu.VMEM((seq, LANES), BF16), pltpu.VMEM((seq, LANES), BF16)]
        + [pltpu.VMEM((tq, LANES), F32) for _ in range(4)],
        compiler_params=_params(("arbitrary", "arbitrary", "arbitrary")),
        name="sb_attn_prompt",
    )(qkv2, qkv2, qkv2)


def _sb_sample_kernel(pt_ref, q_ref, kn_ref, vn_ref, *rest, n_steps, pps, n_heads, d_head, nq, scale):
    del pt_ref
    kp_refs, vp_refs = rest[:pps], rest[pps:2 * pps]
    oin_ref, o_ref, qbd, carry, acc = rest[2 * pps:]
    del oin_ref
    p = pl.program_id(1)
    dm = n_heads * d_head
    nrow = n_heads * nq
    later = _later_matrix(LANES)

    @pl.when(p == 0)
    def _():
        row = lax.broadcasted_iota(jnp.int32, (nrow, dm), 0)
        col = lax.broadcasted_iota(jnp.int32, (nrow, dm), 1)
        qt = jnp.concatenate([q_ref[...] * (-scale)] * n_heads, axis=0)
        qbd[...] = jnp.where(col // d_head == row // nq, qt, 0.0).astype(BF16)
        pad = jnp.zeros((LANES - nq, dm), F32)
        kn = jnp.concatenate([kn_ref[...], pad], axis=0).astype(BF16)
        vn = jnp.concatenate([vn_ref[...], pad], axis=0).astype(BF16)
        r = lax.broadcasted_iota(jnp.int32, (nrow, LANES), 0)
        j = lax.broadcasted_iota(jnp.int32, (nrow, LANES), 1)
        mask = j < r % nq
        zn = _dot_nt(qbd[...], kn)
        w, c = _stick_block(zn, _log_stay(zn, mask), jnp.zeros((nrow, LANES), F32), later, mask)
        acc[...] = _dot(w.astype(BF16), vn)
        carry[...] = c

    for i in range(pps):
        zn = _dot(qbd[...], kp_refs[i][...].astype(BF16))
        w, c = _stick_block(zn, _log_stay(zn, None), carry[...], later, None)
        acc[...] += _dot_nt(w.astype(BF16), vp_refs[i][...].astype(BF16))
        carry[...] = c

    @pl.when(p == n_steps - 1)
    def _():
        row = lax.broadcasted_iota(jnp.int32, (nrow, dm), 0)
        col = lax.broadcasted_iota(jnp.int32, (nrow, dm), 1)
        comb = jnp.where(col // d_head == row // nq, acc[...], 0.0)
        o = comb[0:nq]
        for h in range(1, n_heads):
            o = o + comb[h * nq:(h + 1) * nq]
        o_ref[...] = o


def sb_attention_sample(o_prompt, qkv2, cache_k, cache_v, page_table, *, n_prompt_rows, n_heads, d_head, nq, layer):
    T, dm = o_prompt.shape
    nseq, n_pages = page_table.shape
    n_layers, n_pool, page = cache_k.shape[:3]
    assert page == LANES and n_heads * nq == LANES and nq == GROUP
    ck = jnp.transpose(cache_k, (0, 1, 3, 4, 2)).reshape(n_layers, n_pool, dm, page)
    cv = jnp.transpose(cache_v, (0, 1, 3, 4, 2)).reshape(n_layers, n_pool, dm, page)
    r0 = n_prompt_rows // nq
    pps = _pick(n_pages, (4, 2, 1))
    n_steps = n_pages // pps

    def page_map(i):
        return lambda b, p, pt: (layer, pt[b * n_pages + n_pages - 1 - (p * pps + i)], 0, 0)

    in_specs = [pl.BlockSpec((nq, dm), lambda b, p, pt: (r0 + b, 0)),
                pl.BlockSpec((nq, dm), lambda b, p, pt: (r0 + b, 1)),
                pl.BlockSpec((nq, dm), lambda b, p, pt: (r0 + b, 2))]
    in_specs += [pl.BlockSpec((None, None, dm, page), page_map(i)) for i in range(pps)] * 2
    in_specs += [pl.BlockSpec(memory_space=pl.ANY)]
    grid_spec = pltpu.PrefetchScalarGridSpec(
        num_scalar_prefetch=1,
        grid=(nseq, n_steps),
        in_specs=in_specs,
        out_specs=pl.BlockSpec((nq, dm), lambda b, p, pt: (r0 + b, 0)),
        scratch_shapes=[pltpu.VMEM((LANES, dm), BF16), pltpu.VMEM((LANES, LANES), F32),
                        pltpu.VMEM((LANES, dm), F32)],
    )
    kern = functools.partial(_sb_sample_kernel, n_steps=n_steps, pps=pps, n_heads=n_heads, d_head=d_head,
                             nq=nq, scale=d_head ** -0.5)
    return pl.pallas_call(
        kern,
        grid_spec=grid_spec,
        out_shape=jax.ShapeDtypeStruct((T, dm), F32),
        input_output_aliases={4 + 2 * pps: 0},
        compiler_params=_params(("arbitrary", "arbitrary")),
        name="sb_attn_sample",
    )(page_table.reshape(-1), qkv2, qkv2, qkv2, *([ck] * pps), *([cv] * pps), o_prompt)


def _gelu_tanh(x):
    return 0.5 * x * (1.0 + jnp.tanh(math.sqrt(2.0 / math.pi) * (x + 0.044715 * x * x * x)))


def _s5_kernel(u_ref, h0r_ref, h0i_ref, wb_ref, ar_ref, ai_ref, wc_ref, d_ref,
               y_ref, hr_ref, hi_ref,
               bur, bui, str_, sti, wbhi, wblo, wcbf, *, nb, tt, n_chunk, slabs_per_step):
    t_idx = pl.program_id(1)
    rows = nb * tt
    dm = u_ref.shape[-1]
    ck = dm // n_chunk
    n_slab = bur.shape[0]
    spc = n_slab // n_chunk
    sk = spc * LANES

    @pl.when((pl.program_id(0) == 0) & (t_idx == 0))
    def _():
        wb = wb_ref[...]
        hi = wb.astype(BF16)
        wbhi[...] = hi
        wblo[...] = (wb - hi.astype(F32)).astype(BF16)
        wcbf[...] = wc_ref[...].astype(BF16)

    @pl.when(t_idx == 0)
    def _():
        str_[...] = h0r_ref[...]
        sti[...] = h0i_ref[...]

    u2 = u_ref[...].reshape(rows, dm)
    for j in range(n_chunk):
        uj = u2[:, j * ck:(j + 1) * ck]
        uh, ul = _split2(uj)
        bu = _dot(uh, wbhi[j]) + _dot(ul, wbhi[j]) + _dot(uh, wblo[j])
        for q in range(spc):
            bur[j * spc + q] = bu[:, q * LANES:(q + 1) * LANES]
            bui[j * spc + q] = bu[:, sk + q * LANES:sk + (q + 1) * LANES]

    rc = min(nb, GROUP)
    for r0 in range(0, nb, rc):
        for s0 in range(0, n_slab, slabs_per_step):
            sl = [slice((s0 + q) * LANES, (s0 + q + 1) * LANES) for q in range(slabs_per_step)]
            ar = [ar_ref[:, c] for c in sl]
            ai = [ai_ref[:, c] for c in sl]
            h0 = tuple(str_[r0:r0 + rc, c] for c in sl) + tuple(sti[r0:r0 + rc, c] for c in sl)

            def step(t, h, r0=r0, s0=s0, ar=ar, ai=ai):
                rows_t = pl.ds(r0 * tt + t, rc, stride=tt)
                out_r, out_i = [], []
                for q in range(slabs_per_step):
                    h_r, h_i = h[q], h[slabs_per_step + q]
                    n_r = ar[q] * h_r - ai[q] * h_i + bur[s0 + q, rows_t, :]
                    n_i = ar[q] * h_i + ai[q] * h_r + bui[s0 + q, rows_t, :]
                    bur[s0 + q, rows_t, :] = n_r
                    bui[s0 + q, rows_t, :] = n_i
                    out_r.append(n_r)
                    out_i.append(n_i)
                return tuple(out_r) + tuple(out_i)

            h = lax.fori_loop(0, tt, step, h0)
            for q, c in enumerate(sl):
                str_[r0:r0 + rc, c] = h[q]
                sti[r0:r0 + rc, c] = h[slabs_per_step + q]

    hr_ref[...] = str_[...]
    hi_ref[...] = sti[...]

    for j in range(n_chunk):
        hcat = jnp.concatenate([bur[j * spc + q] for q in range(spc)] + [bui[j * spc + q] for q in range(spc)], axis=1)
        yj = _dot(hcat.astype(BF16), wcbf[j]) + d_ref[:, j * ck:(j + 1) * ck] * u2[:, j * ck:(j + 1) * ck]
        y_ref[:, :, j * ck:(j + 1) * ck] = _gelu_tanh(yj).reshape(nb, tt, ck)


def s5_scan(u3, h0r, h0i, wb, ar, ai, wc, d_skip, *, nb, tt, name):
    B, T, D = u3.shape
    S = h0r.shape[1]
    n_chunk = wb.shape[0]
    kern = functools.partial(_s5_kernel, nb=nb, tt=tt, n_chunk=n_chunk, slabs_per_step=8)
    full = lambda a: pl.BlockSpec(a.shape, lambda b, t: (0,) * a.ndim)
    return pl.pallas_call(
        kern,
        grid=(B // nb, T // tt),
        in_specs=[pl.BlockSpec((nb, tt, D), lambda b, t: (b, t, 0)),
                  pl.BlockSpec((nb, S), lambda b, t: (b, 0)),
                  pl.BlockSpec((nb, S), lambda b, t: (b, 0)),
                  full(wb), full(ar), full(ai), full(wc), full(d_skip)],
        out_specs=[pl.BlockSpec((nb, tt, D), lambda b, t: (b, t, 0)),
                   pl.BlockSpec((nb, S), lambda b, t: (b, 0)),
                   pl.BlockSpec((nb, S), lambda b, t: (b, 0))],
        out_shape=[jax.ShapeDtypeStruct((B, T, D), F32),
                   jax.ShapeDtypeStruct((B, S), F32),
                   jax.ShapeDtypeStruct((B, S), F32)],
        scratch_shapes=[pltpu.VMEM((S // LANES, nb * tt, LANES), F32), pltpu.VMEM((S // LANES, nb * tt, LANES), F32),
                        pltpu.VMEM((nb, S), F32), pltpu.VMEM((nb, S), F32),
                        pltpu.VMEM(wb.shape, BF16), pltpu.VMEM(wb.shape, BF16), pltpu.VMEM(wc.shape, BF16)],
        compiler_params=_params(("arbitrary", "arbitrary")),
        name=name,
    )(u3, h0r, h0i, wb, ar, ai, wc, d_skip)


def s5_tables(log_dt, a_re, a_im, b_re, b_im, c_re, c_im, n_chunk):
    ng, p = a_re.shape
    ch = b_re.shape[-1]
    dt = jnp.exp(log_dt)[:, None]
    mag, ang = jnp.exp(dt * a_re), dt * a_im
    abr, abi = mag * jnp.cos(ang), mag * jnp.sin(ang)
    den = a_re * a_re + a_im * a_im
    fr = ((abr - 1.0) * a_re + abi * a_im) / den
    fi = (abi * a_re - (abr - 1.0) * a_im) / den
    bbr = fr[..., None] * b_re - fi[..., None] * b_im
    bbi = fr[..., None] * b_im + fi[..., None] * b_re
    gpc = ng // n_chunk
    eye = jnp.eye(gpc, dtype=F32)

    def in_proj(bb):
        b4 = bb.reshape(n_chunk, gpc, p, ch)
        return jnp.einsum("jgpc,gh->jgchp", b4, eye).reshape(n_chunk, gpc * ch, gpc * p)

    def out_proj(cc):
        c4 = cc.reshape(n_chunk, gpc, ch, p)
        return jnp.einsum("jgcp,gh->jgphc", c4, eye).reshape(n_chunk, gpc * p, gpc * ch)

    wb = jnp.concatenate([in_proj(bbr), in_proj(bbi)], axis=2)
    wc = jnp.concatenate([out_proj(c_re), -out_proj(c_im)], axis=1)
    return wb, abr.reshape(1, ng * p), abi.reshape(1, ng * p), wc


def _router_kernel(x_ref, g_ref, sh_ref, sc_ref, w_ref, b_ref, h_ref, gate_ref, idx_ref, rank_ref, cnt_ref,
                   whi, wlo, count, *, bb, rows):
    @pl.when(pl.program_id(0) == 0)
    def _():
        w = w_ref[...]
        hi = w.astype(BF16)
        whi[...] = hi
        wlo[...] = (w - hi.astype(F32)).astype(BF16)
        count[...] = jnp.zeros(count.shape, F32)

    x = x_ref[...]
    ms = jnp.mean(x * x, axis=-1, keepdims=True)
    x = x * lax.rsqrt(ms + EPS) * g_ref[...]
    x = x * (1.0 + sc_ref[...]) + sh_ref[...]
    h_ref[...] = x
    tm = bb * rows
    x2 = x.reshape(tm, x.shape[-1])
    xh, xl = _split2(x2)
    logits = _dot(xh, whi[...]) + _dot(xl, whi[...]) + _dot(xh, wlo[...]) + b_ref[...]
    ne = logits.shape[1]
    lane = lax.broadcasted_iota(jnp.int32, logits.shape, 1).astype(F32)
    kcol = lax.broadcasted_iota(jnp.int32, (tm, TOP_K), 1)
    vals = jnp.zeros((tm, TOP_K), F32)
    idxs = jnp.zeros((tm, TOP_K), F32)
    cur = logits
    hot = []
    for k in range(TOP_K):
        mx = jnp.max(cur, axis=1, keepdims=True)
        am = jnp.min(jnp.where(cur == mx, lane, float(ne)), axis=1, keepdims=True)
        vals = jnp.where(kcol == k, mx, vals)
        idxs = jnp.where(kcol == k, am, idxs)
        sel = lane == am
        hot.append(jnp.where(sel, 1.0, 0.0))
        cur = jnp.where(sel, -jnp.inf, cur)
    e = jnp.exp(vals - vals[:, 0:1])
    gate_ref[...] = e / jnp.sum(e, axis=1, keepdims=True)
    idx_ref[...] = idxs.astype(jnp.int32)
    cnt = hot[0] + hot[1] + hot[2] + hot[3]
    r = lax.broadcasted_iota(jnp.int32, (tm, tm), 0)
    c = lax.broadcasted_iota(jnp.int32, (tm, tm), 1)
    before = jnp.where(c < r, 1.0, 0.0).astype(BF16)
    base = _dot(before, cnt.astype(BF16)) + count[...]
    ranks = jnp.zeros((tm, TOP_K), F32)
    for k in range(TOP_K):
        ranks = jnp.where(kcol == k, jnp.sum(hot[k] * base, axis=1, keepdims=True), ranks)
    rank_ref[...] = ranks.astype(jnp.int32)
    count[...] += jnp.sum(cnt, axis=0, keepdims=True)
    cnt_ref[...] = count[...]


def moe_router(x3, gain, layer, modexp, w_router, b_router):
    G, R, K = x3.shape
    E = w_router.shape[-1]
    bb = _pick(G, (64, 32, 16, 8, 4, 2, 1))
    tm = bb * R
    T = G * R
    kern = functools.partial(_router_kernel, bb=bb, rows=R)
    return pl.pallas_call(
        kern,
        grid=(G // bb,),
        in_specs=[pl.BlockSpec((bb, R, K), lambda g: (g, 0, 0)),
                  pl.BlockSpec((None, 1, K), lambda g: (layer, 0, 0)),
                  pl.BlockSpec((None, bb, 1, K), lambda g: (3, g, 0, 0)),
                  pl.BlockSpec((None, bb, 1, K), lambda g: (4, g, 0, 0)),
                  pl.BlockSpec((None, K, E), lambda g: (layer, 0, 0)),
                  pl.BlockSpec((None, 1, E), lambda g: (layer, 0, 0))],
        out_specs=[pl.BlockSpec((bb, R, K), lambda g: (g, 0, 0)),
                   pl.BlockSpec((tm, TOP_K), lambda g: (g, 0)),
                   pl.BlockSpec((tm, TOP_K), lambda g: (g, 0)),
                   pl.BlockSpec((tm, TOP_K), lambda g: (g, 0)),
                   pl.BlockSpec((1, E), lambda g: (0, 0))],
        out_shape=[jax.ShapeDtypeStruct((G, R, K), F32),
                   jax.ShapeDtypeStruct((T, TOP_K), F32),
                   jax.ShapeDtypeStruct((T, TOP_K), jnp.int32),
                   jax.ShapeDtypeStruct((T, TOP_K), jnp.int32),
                   jax.ShapeDtypeStruct((1, E), F32)],
        scratch_shapes=[pltpu.VMEM((K, E), BF16), pltpu.VMEM((K, E), BF16), pltpu.VMEM((1, E), F32)],
        compiler_params=_params(("arbitrary",)),
        name="moe_router",
    )(x3, gain.reshape(gain.shape[0], 1, K), modexp, modexp, w_router,
      b_router.reshape(b_router.shape[0], 1, E))


def _dispatch_kernel(dest_ref, h_hbm, xs_in, xs_hbm, sem, *, tm):
    del xs_in
    i = pl.program_id(0)
    n = pl.num_programs(0)

    def wait(s):
        pltpu.make_async_copy(h_hbm.at[pl.ds(0, tm * TOP_K), :], xs_hbm.at[pl.ds(0, tm * TOP_K), :], sem.at[s]).wait()

    def issue(r, c):
        t = i * tm + r
        for k in range(TOP_K):
            pltpu.make_async_copy(h_hbm.at[pl.ds(t, 1), :], xs_hbm.at[pl.ds(dest_ref[t * TOP_K + k], 1), :],
                                  sem.at[i % 2]).start()
        return c

    lax.fori_loop(0, tm, issue, 0)

    @pl.when(i > 0)
    def _():
        wait((i - 1) % 2)

    @pl.when(i == n - 1)
    def _():
        wait(i % 2)


def moe_dispatch(h2, dest, n_slots):
    T, D = h2.shape
    tm = _pick(T, (256, 128, 64, 32, 16, 8))
    xs0 = jnp.zeros((n_slots, D), F32)
    grid_spec = pltpu.PrefetchScalarGridSpec(
        num_scalar_prefetch=1,
        grid=(T // tm,),
        in_specs=[pl.BlockSpec(memory_space=pl.ANY), pl.BlockSpec(memory_space=pl.ANY)],
        out_specs=pl.BlockSpec(memory_space=pl.ANY),
        scratch_shapes=[pltpu.SemaphoreType.DMA((2,))],
    )
    return pl.pallas_call(
        functools.partial(_dispatch_kernel, tm=tm),
        grid_spec=grid_spec,
        out_shape=jax.ShapeDtypeStruct((n_slots, D), F32),
        input_output_aliases={2: 0},
        compiler_params=_params(("arbitrary",)),
        name="moe_dispatch",
    )(dest, h2, xs0)


def _expert_kernel(bexp_ref, nblk_ref, x_ref, wgu_ref, bgu_ref, wd_ref, bd_ref, o_ref, wgu_bf, wd_bf, *, d_ff):
    i = pl.program_id(0)
    n_used = nblk_ref[0]

    @pl.when(i < n_used)
    def _():
        changed = jnp.logical_or(i == 0, bexp_ref[i] != bexp_ref[jnp.maximum(i - 1, 0)])

        @pl.when(changed)
        def _():
            wgu_bf[...] = wgu_ref[...].astype(BF16)
            wd_bf[...] = wd_ref[...].astype(BF16)

        gu = _dot(x_ref[...].astype(BF16), wgu_bf[...]) + bgu_ref[...]
        gate = jnp.minimum(gu[:, :d_ff], SWIGLU_LIMIT)
        up = jnp.clip(gu[:, d_ff:], -SWIGLU_LIMIT, SWIGLU_LIMIT)
        act = (up + 1.0) * gate * _sigmoid(SWIGLU_ALPHA * gate)
        o_ref[...] = _dot(act.astype(BF16), wd_bf[...]) + bd_ref[...]

    @pl.when(i >= n_used)
    def _():
        o_ref[...] = jnp.zeros(o_ref.shape, F32)


def moe_experts(xs, block_expert, n_used, w_gu, b_gu, w_down, b_down, *, layer, blk):
    n_slots, D = xs.shape
    n_blocks = n_slots // blk
    E, _, F2 = w_gu.shape[1:]
    d_ff = F2 // 2
    grid_spec = pltpu.PrefetchScalarGridSpec(
        num_scalar_prefetch=2,
        grid=(n_blocks,),
        in_specs=[pl.BlockSpec((blk, D), lambda i, be, nu: (jnp.minimum(i, nu[0] - 1), 0)),
                  pl.BlockSpec((None, None, D, F2), lambda i, be, nu: (layer, be[i], 0, 0)),
                  pl.BlockSpec((None, None, 1, F2), lambda i, be, nu: (layer, be[i], 0, 0)),
                  pl.BlockSpec((None, None, d_ff, D), lambda i, be, nu: (layer, be[i], 0, 0)),
                  pl.BlockSpec((None, None, 1, D), lambda i, be, nu: (layer, be[i], 0, 0))],
        out_specs=pl.BlockSpec((blk, D), lambda i, be, nu: (i, 0)),
        scratch_shapes=[pltpu.VMEM((D, F2), BF16), pltpu.VMEM((d_ff, D), BF16)],
    )
    return pl.pallas_call(
        functools.partial(_expert_kernel, d_ff=d_ff),
        grid_spec=grid_spec,
        out_shape=jax.ShapeDtypeStruct((n_slots, D), F32),
        compiler_params=_params(("arbitrary",)),
        name="moe_experts",
    )(block_expert, n_used, xs, w_gu, b_gu.reshape(b_gu.shape[0], E, 1, F2), w_down,
      b_down.reshape(b_down.shape[0], E, 1, D))


def _combine_kernel(dest_ref, yb_hbm, gates_ref, res_ref, gate_ref, o_ref, buf, sem, *, tm, bb, rows):
    i = pl.program_id(0)
    n = pl.num_programs(0)
    slot = i % 2

    def gather(tile, s):
        def issue(r, c):
            for k in range(TOP_K):
                src = dest_ref[(tile * tm + r) * TOP_K + k]
                pltpu.make_async_copy(yb_hbm.at[pl.ds(src, 1), :], buf.at[s, k, pl.ds(r, 1), :], sem.at[s]).start()
            return c
        lax.fori_loop(0, tm, issue, 0)

    @pl.when(i == 0)
    def _():
        gather(0, 0)

    @pl.when(i + 1 < n)
    def _():
        gather(i + 1, 1 - slot)

    for k in range(TOP_K):
        pltpu.make_async_copy(yb_hbm.at[pl.ds(0, tm), :], buf.at[slot, k], sem.at[slot]).wait()
    g = gates_ref[...]
    y = g[:, 0:1] * buf[slot, 0]
    for k in range(1, TOP_K):
        y = y + g[:, k:k + 1] * buf[slot, k]
    o_ref[...] = res_ref[...] + gate_ref[...] * y.reshape(bb, rows, y.shape[-1])


def moe_combine(yb, dest, gates, xres3, modexp):
    G, R, D = xres3.shape
    bb = _pick(G, (16, 8, 4, 2, 1))
    tm = bb * R
    grid_spec = pltpu.PrefetchScalarGridSpec(
        num_scalar_prefetch=1,
        grid=(G // bb,),
        in_specs=[pl.BlockSpec(memory_space=pl.ANY),
                  pl.BlockSpec((tm, TOP_K), lambda g, d: (g, 0)),
                  pl.BlockSpec((bb, R, D), lambda g, d: (g, 0, 0)),
                  pl.BlockSpec((None, bb, 1, D), lambda g, d: (5, g, 0, 0))],
        out_specs=pl.BlockSpec((bb, R, D), lambda g, d: (g, 0, 0)),
        scratch_shapes=[pltpu.VMEM((2, TOP_K, tm, D), F32), pltpu.SemaphoreType.DMA((2,))],
    )
    return pl.pallas_call(
        functools.partial(_combine_kernel, tm=tm, bb=bb, rows=R),
        grid_spec=grid_spec,
        out_shape=jax.ShapeDtypeStruct((G, R, D), F32),
        compiler_params=_params(("arbitrary",)),
        name="moe_combine",
    )(dest, yb, gates, xres3, modexp)


def moe_layer(x3, modexp, layer, norm_ffn, w_router, b_router, w_gu, b_gu, w_down, b_down, blk):
    G, R, D = x3.shape
    T = G * R
    E = w_router.shape[-1]
    h3, gates, idx, rank, counts = moe_router(x3, norm_ffn, layer, modexp, w_router, b_router)
    counts = counts.reshape(E).astype(jnp.int32)
    padded = (counts + blk - 1) // blk * blk
    pad_end = jnp.cumsum(padded)
    pad_start = pad_end - padded
    experts = jnp.arange(E, dtype=jnp.int32)
    dest = rank + jnp.sum(jnp.where(idx[..., None] == experts, pad_start, 0), axis=-1)
    dest = dest.reshape(-1).astype(jnp.int32)
    n_blocks = -(-(T * TOP_K) // blk) + E
    block_start = jnp.arange(n_blocks, dtype=jnp.int32) * blk
    block_expert = jnp.minimum(jnp.sum(pad_end[None, :] <= block_start[:, None], axis=1), E - 1).astype(jnp.int32)
    n_used = (pad_end[-1:] // blk).astype(jnp.int32)
    xs = moe_dispatch(h3.reshape(T, D), dest, n_blocks * blk)
    yb = moe_experts(xs, block_expert, n_used, w_gu, b_gu, w_down, b_down, layer=layer, blk=blk)
    return moe_combine(yb, dest, gates, x3, modexp)


def kernel(x_prompt, x_sample, cache_a_k, cache_a_v, state_b_re, state_b_im, cache_c_k, cache_c_v, page_table, c_prompt, c_sample, w_ada, b_ada, norm_mix, norm_ffn, a_w_in, a_q_norm, a_k_norm, a_lambda, a_sub_norm, a_w_out, b_log_dt, b_a_re, b_a_im, b_b_re, b_b_im, b_c_re, b_c_im, b_d, b_w_glu, b_b_glu, c_w_in, c_w_out, moe_w_router, moe_b_router, moe_w_gu, moe_b_gu, moe_w_down, moe_b_down):
    batch, seq, d = x_prompt.shape
    nseq, dec_seq, _ = x_sample.shape
    assert dec_seq == GROUP and seq % GROUP == 0
    depth = w_ada.shape[0]
    n_pages, page = page_table.shape[1], cache_a_k.shape[2]
    past = n_pages * page
    tp, ts = batch * seq, nseq * dec_seq
    T = tp + ts
    G = T // GROUP
    gp = tp // GROUP
    h_a, d_ha = cache_a_k.shape[3], cache_a_k.shape[5]
    h_c, d_hc = cache_c_k.shape[3], cache_c_k.shape[4]
    ssm_g, ssm_p = b_a_re.shape[1], b_a_re.shape[2]
    n_state = ssm_g * ssm_p

    x3 = jnp.concatenate([x_prompt.reshape(gp, GROUP, d), x_sample], axis=0)

    nc = batch + nseq
    ncp = -(-nc // GROUP) * GROUP
    c_all = jnp.concatenate([c_prompt, c_sample, jnp.zeros((ncp - nc, d), F32)], axis=0).reshape(1, ncp, d)
    modexps = []
    for i in range(depth):
        m = fused_matmul(c_all, w_ada, b_ada, layer=i, silu_in=True, passes=3, name="ada")[0]
        m = m.reshape(ncp, 6, d).transpose(1, 0, 2)
        mexp = jnp.concatenate([jnp.repeat(m[:, :batch], seq // GROUP, axis=1), m[:, batch:nc]], axis=1)
        modexps.append(mexp.reshape(6, G, 1, d))

    pos = jnp.concatenate([jnp.tile(jnp.arange(seq, dtype=jnp.int32), batch),
                           jnp.tile(past + jnp.arange(dec_seq, dtype=jnp.int32), nseq)])
    freqs = ROPE_THETA ** (-jnp.arange(0, d_ha, 2, dtype=F32) / d_ha)
    ang = pos.astype(F32)[:, None] * freqs[None, :]
    cos, sin = jnp.cos(ang), jnp.sin(ang)
    reps = LANES // d_ha
    cos_t = jnp.tile(jnp.concatenate([cos, cos], axis=1), (1, reps))
    sin_t = jnp.tile(jnp.concatenate([-sin, sin], axis=1), (1, reps))

    zeros_state = jnp.zeros((batch, n_state), F32)
    moe_blk = 256 if T * TOP_K >= 32768 else 128
    new_a, new_b, new_c = [], [], []
    for i in range(depth):
        kind, j = i % N_MIXERS, i // N_MIXERS
        mexp = modexps[i]
        if kind == 0:
            lam_init = 0.8 - 0.6 * math.exp(-0.3 * i)
            nqk = h_a * 2 * d_ha
            qkv3 = fused_matmul(x3, a_w_in, layer=j, norm=(norm_mix, i, mexp, 0, 1), name="a_qkv")
            qkv2 = qkv3.reshape(T, 3 * nqk)
            gains = jnp.stack([jnp.tile(a_q_norm[j], reps) * (d_ha ** -0.5), jnp.tile(a_k_norm[j], reps)])
            qk2 = qknorm_rope(qkv2, gains.reshape(2, 1, LANES), cos_t, sin_t, 2 * nqk, d_ha)
            o2 = diff_attention_prompt(qk2, qkv2, a_sub_norm, a_lambda, batch=batch, seq=seq, n_heads=h_a,
                                       d_head=d_ha, lam_init=lam_init, layer=j)
            o2 = diff_attention_sample(o2, qk2, qkv2, cache_a_k, cache_a_v, page_table, a_sub_norm, a_lambda,
                                       n_prompt_rows=tp, n_heads=h_a, d_head=d_ha, nq=dec_seq,
                                       lam_init=lam_init, layer=j)
            x3 = fused_matmul(o2.reshape(G, GROUP, nqk), a_w_out, layer=j, resid=(x3, mexp, 2), name="a_out")
            k2, v2 = qk2[:, nqk:], qkv2[:, 2 * nqk:]
            new_a.append((k2[:tp].reshape(batch, seq, h_a, 2, d_ha), v2[:tp].reshape(batch, seq, h_a, 2 * d_ha),
                          k2[tp:].reshape(nseq, dec_seq, h_a, 2, d_ha), v2[tp:].reshape(nseq, dec_seq, h_a, 2 * d_ha)))
        elif kind == 1:
            u3 = mod_norm(x3, norm_mix, i, mexp, 0, 1)
            wb, ar, ai, wc = s5_tables(b_log_dt[j], b_a_re[j], b_a_im[j], b_b_re[j], b_b_im[j],
                                       b_c_re[j], b_c_im[j], ssm_g // GROUP)
            d_skip = b_d[j].reshape(1, d)
            y_p, hpr, hpi = s5_scan(u3[:gp].reshape(batch, seq, d), zeros_state, zeros_state, wb, ar, ai, wc,
                                    d_skip, nb=batch, tt=_pick(seq, (64, 32, 16, 8)), name="s5_prompt")
            y_s, hsr, hsi = s5_scan(u3[gp:], state_b_re[j].reshape(nseq, n_state), state_b_im[j].reshape(nseq, n_state),
                                    wb, ar, ai, wc, d_skip, nb=_pick(nseq, (32, 16, 8)), tt=dec_seq, name="s5_sample")
            y3 = jnp.concatenate([y_p.reshape(gp, GROUP, d), y_s], axis=0)
            x3 = fused_matmul(y3, b_w_glu, b_b_glu, layer=j, glu=True, resid=(x3, mexp, 2), name="b_glu")
            st = lambda a, n: a.reshape(n, ssm_g, ssm_p)
            new_b.append((st(hpr, batch), st(hpi, batch), st(hsr, nseq), st(hsi, nseq)))
        else:
            dm = h_c * d_hc
            qkv3 = fused_matmul(x3, c_w_in, layer=j, norm=(norm_mix, i, mexp, 0, 1), name="c_qkv")
            qkv2 = qkv3.reshape(T, 3 * dm)
            o2 = sb_attention_prompt(qkv2, batch=batch, seq=seq, n_heads=h_c, d_head=d_hc)
            o2 = sb_attention_sample(o2, qkv2, cache_c_k, cache_c_v, page_table, n_prompt_rows=tp,
                                     n_heads=h_c, d_head=d_hc, nq=dec_seq, layer=j)
            x3 = fused_matmul(o2.reshape(G, GROUP, dm), c_w_out, layer=j, resid=(x3, mexp, 2), name="c_out")
            k2, v2 = qkv2[:, dm:2 * dm], qkv2[:, 2 * dm:]
            new_c.append((k2[:tp].reshape(batch, seq, h_c, d_hc), v2[:tp].reshape(batch, seq, h_c, d_hc),
                          k2[tp:].reshape(nseq, dec_seq, h_c, d_hc), v2[tp:].reshape(nseq, dec_seq, h_c, d_hc)))
        x3 = moe_layer(x3, mexp, i, norm_ffn, moe_w_router, moe_b_router, moe_w_gu, moe_b_gu,
                       moe_w_down, moe_b_down, moe_blk)

    outs = [x3[:gp].reshape(batch, seq, d), x3[gp:]]
    for group in (new_a, new_b, new_c):
        for k in range(4):
            outs.append(jnp.stack([entry[k] for entry in group]))
    return tuple(outs)
```

```python
import functools
import math

import jax
import jax.numpy as jnp
from jax import lax
from jax.experimental import pallas as pl
from jax.experimental.pallas import tpu as pltpu

F32 = jnp.float32
BF16 = jnp.bfloat16

EPS = 1e-6
NEG_INF = -1e30
ROPE_THETA = 10000.0
TOP_K = 4
SWIGLU_LIMIT = 7.0
SWIGLU_ALPHA = 1.702
N_MIXERS = 3
GROUP = 8
LANES = 128
VMEM_LIMIT = 56 * 1024 * 1024


def _params(sem, vmem=VMEM_LIMIT):
    return pltpu.CompilerParams(dimension_semantics=sem, vmem_limit_bytes=vmem)


def _pick(n, cands):
    for c in cands:
        if n % c == 0:
            return c
    raise ValueError(f"no tile for {n} in {cands}")


def _split2(x):
    hi = x.astype(BF16)
    lo = (x - hi.astype(F32)).astype(BF16)
    return hi, lo


def _sigmoid(x):
    return 1.0 / (1.0 + jnp.exp(-x))


def _dot(a, b):
    return jnp.dot(a, b, preferred_element_type=F32)


def _dot_nt(a, b):
    return lax.dot_general(a, b, (((1,), (1,)), ((), ())), preferred_element_type=F32)


def _mm_kernel(*refs, norm, silu_in, glu, has_bias, resid, passes, bb, rows):
    it = iter(refs)
    x_ref = next(it)
    if norm:
        g_ref, sh_ref, sc_ref = next(it), next(it), next(it)
    w_refs = [next(it)] + ([next(it)] if glu else [])
    b_refs = ([next(it)] + ([next(it)] if glu else [])) if has_bias else []
    if resid:
        res_ref, gate_ref = next(it), next(it)
    o_ref = next(it)
    whi = [next(it) for _ in w_refs]
    wlo = [next(it) for _ in w_refs] if passes == 3 else []

    @pl.when(pl.program_id(1) == 0)
    def _():
        for i, w_ref in enumerate(w_refs):
            w = w_ref[...]
            hi = w.astype(BF16)
            whi[i][...] = hi
            if passes == 3:
                wlo[i][...] = (w - hi.astype(F32)).astype(BF16)

    x = x_ref[...]
    if silu_in:
        x = x * _sigmoid(x)
    if norm:
        ms = jnp.mean(x * x, axis=-1, keepdims=True)
        x = x * lax.rsqrt(ms + EPS) * g_ref[...]
        x = x * (1.0 + sc_ref[...]) + sh_ref[...]
    x2 = x.reshape(bb * rows, x.shape[-1])
    xh = x2.astype(BF16)
    xl = (x2 - xh.astype(F32)).astype(BF16) if passes == 3 else None

    def mm(i):
        acc = _dot(xh, whi[i][...])
        if passes == 3:
            acc = acc + _dot(xl, whi[i][...]) + _dot(xh, wlo[i][...])
        if has_bias:
            acc = acc + b_refs[i][...]
        return acc

    z = mm(0)
    if glu:
        z = z * _sigmoid(mm(1))
    z3 = z.reshape(bb, rows, z.shape[-1])
    if resid:
        z3 = res_ref[...] + gate_ref[...] * z3
    o_ref[...] = z3


def fused_matmul(x3, w, b=None, *, layer=0, norm=None, silu_in=False, glu=False,
                 resid=None, passes=1, tn=None, name="mm"):
    G, R, K = x3.shape
    N = w.shape[-1]
    n_out = N // 2 if glu else N
    bb = _pick(G, (64, 32, 16, 8, 4, 2, 1))
    tn = tn or min(n_out, 1024)
    assert n_out % tn == 0
    half = n_out // tn

    in_specs = [pl.BlockSpec((bb, R, K), lambda n, g: (g, 0, 0))]
    args = [x3]
    if norm is not None:
        gain, gl, modexp, i_sh, i_sc = norm
        gain3 = gain.reshape(gain.shape[0], 1, K)
        in_specs += [pl.BlockSpec((None, 1, K), lambda n, g: (gl, 0, 0)),
                     pl.BlockSpec((None, bb, 1, K), lambda n, g: (i_sh, g, 0, 0)),
                     pl.BlockSpec((None, bb, 1, K), lambda n, g: (i_sc, g, 0, 0))]
        args += [gain3, modexp, modexp]
    in_specs.append(pl.BlockSpec((None, K, tn), lambda n, g: (layer, 0, n)))
    args.append(w)
    if glu:
        in_specs.append(pl.BlockSpec((None, K, tn), lambda n, g: (layer, 0, n + half)))
        args.append(w)
    if b is not None:
        b3 = b.reshape(b.shape[0], 1, N)
        in_specs.append(pl.BlockSpec((None, 1, tn), lambda n, g: (layer, 0, n)))
        args.append(b3)
        if glu:
            in_specs.append(pl.BlockSpec((None, 1, tn), lambda n, g: (layer, 0, n + half)))
            args.append(b3)
    if resid is not None:
        xres, modexp_r, i_g = resid
        in_specs += [pl.BlockSpec((bb, R, tn), lambda n, g: (g, 0, n)),
                     pl.BlockSpec((None, bb, 1, tn), lambda n, g: (i_g, g, 0, n))]
        args += [xres, modexp_r]
    nw = 2 if glu else 1
    scratch = [pltpu.VMEM((K, tn), BF16) for _ in range(nw * (2 if passes == 3 else 1))]
    kern = functools.partial(_mm_kernel, norm=norm is not None, silu_in=silu_in, glu=glu,
                             has_bias=b is not None, resid=resid is not None, passes=passes,
                             bb=bb, rows=R)
    return pl.pallas_call(
        kern,
        grid=(n_out // tn, G // bb),
        in_specs=in_specs,
        out_specs=pl.BlockSpec((bb, R, tn), lambda n, g: (g, 0, n)),
        out_shape=jax.ShapeDtypeStruct((G, R, n_out), F32),
        scratch_shapes=scratch,
        compiler_params=_params(("arbitrary", "arbitrary")),
        name=name,
    )(*args)


def _mod_norm_kernel(x_ref, g_ref, sh_ref, sc_ref, o_ref):
    x = x_ref[...]
    ms = jnp.mean(x * x, axis=-1, keepdims=True)
    x = x * lax.rsqrt(ms + EPS) * g_ref[...]
    o_ref[...] = x * (1.0 + sc_ref[...]) + sh_ref[...]


def mod_norm(x3, gain, layer, modexp, i_sh, i_sc):
    G, R, K = x3.shape
    bb = _pick(G, (64, 32, 16, 8, 4, 2, 1))
    return pl.pallas_call(
        _mod_norm_kernel,
        grid=(G // bb,),
        in_specs=[pl.BlockSpec((bb, R, K), lambda g: (g, 0, 0)),
                  pl.BlockSpec((None, 1, K), lambda g: (layer, 0, 0)),
                  pl.BlockSpec((None, bb, 1, K), lambda g: (i_sh, g, 0, 0)),
                  pl.BlockSpec((None, bb, 1, K), lambda g: (i_sc, g, 0, 0))],
        out_specs=pl.BlockSpec((bb, R, K), lambda g: (g, 0, 0)),
        out_shape=jax.ShapeDtypeStruct((G, R, K), F32),
        compiler_params=_params(("parallel",)),
        name="mod_norm",
    )(x3, gain.reshape(gain.shape[0], 1, K), modexp, modexp)


def _qknorm_rope_kernel(x_ref, g_ref, cos_ref, sin_ref, o_ref, *, d_head):
    x = x_ref[...]
    r = lax.broadcasted_iota(jnp.int32, (LANES, LANES), 0) // d_head
    c = lax.broadcasted_iota(jnp.int32, (LANES, LANES), 1) // d_head
    seg = jnp.where(r == c, 1.0, 0.0).astype(BF16)
    hi, lo = _split2(x * x)
    ss = _dot(hi, seg) + _dot(lo, seg)
    y = x * lax.rsqrt(ss * (1.0 / d_head) + EPS) * g_ref[...]
    lane = lax.broadcasted_iota(jnp.int32, x.shape, 1)
    half = d_head // 2
    rot = jnp.where(lane % d_head < half, pltpu.roll(y, LANES - half, axis=1), pltpu.roll(y, half, axis=1))
    o_ref[...] = y * cos_ref[...] + rot * sin_ref[...]


def qknorm_rope(qkv2, gains, cos_t, sin_t, n_qk_cols, d_head):
    T = qkv2.shape[0]
    tm = _pick(T, (1024, 512, 256, 128, 64, 32, 16, 8))
    ncb = n_qk_cols // LANES
    per = ncb // 2
    return pl.pallas_call(
        functools.partial(_qknorm_rope_kernel, d_head=d_head),
        grid=(T // tm, ncb),
        in_specs=[pl.BlockSpec((tm, LANES), lambda i, j: (i, j)),
                  pl.BlockSpec((None, 1, LANES), lambda i, j: (j // per, 0, 0)),
                  pl.BlockSpec((tm, LANES), lambda i, j: (i, 0)),
                  pl.BlockSpec((tm, LANES), lambda i, j: (i, 0))],
        out_specs=pl.BlockSpec((tm, LANES), lambda i, j: (i, j)),
        out_shape=jax.ShapeDtypeStruct((T, n_qk_cols), F32),
        compiler_params=_params(("parallel", "parallel")),
        name="qknorm_rope",
    )(qkv2, gains, cos_t, sin_t)


def _lambda_value(alam_ref, lam_init):
    a = alam_ref[...]
    s1 = jnp.sum(a[0:1] * a[1:2], axis=1, keepdims=True)
    s2 = jnp.sum(a[2:3] * a[3:4], axis=1, keepdims=True)
    return jnp.exp(s1) - jnp.exp(s2) + lam_init


def _softmax_update(s, v_bf, m_ref, l_ref, acc_ref, reps_s, reps_acc):
    m_prev = m_ref[...]
    m_new = jnp.maximum(m_prev, jnp.max(s, axis=1, keepdims=True))
    alpha = jnp.exp(m_prev - m_new)
    m_b = m_new if reps_s == 1 else jnp.concatenate([m_new] * reps_s, axis=1)
    p = jnp.exp(s - m_b)
    l_ref[...] = alpha * l_ref[...] + jnp.sum(p, axis=1, keepdims=True)
    a_b = alpha if reps_acc == 1 else jnp.concatenate([alpha] * reps_acc, axis=1)
    acc_ref[...] = a_b * acc_ref[...] + _dot(p.astype(BF16), v_bf)
    m_ref[...] = m_new


def _diff_prompt_kernel(q_ref, k_ref, v_ref, g_ref, alam_ref, o_ref,
                        kbf, vbf, m0, l0, a0, m1, l1, a1, *, tq, d_head, lam_init):
    qi = pl.program_id(2)

    @pl.when(qi == 0)
    def _():
        kbf[...] = k_ref[...].astype(BF16)
        vbf[...] = v_ref[...].astype(BF16)

    q = q_ref[...]
    lane = lax.broadcasted_iota(jnp.int32, q.shape, 1)
    qc = (jnp.where(lane < d_head, q, 0.0).astype(BF16), jnp.where(lane >= d_head, q, 0.0).astype(BF16))
    stats = ((m0, l0, a0), (m1, l1, a1))
    for m, l, a in stats:
        m[...] = jnp.full(m.shape, NEG_INF, F32)
        l[...] = jnp.zeros(l.shape, F32)
        a[...] = jnp.zeros(a.shape, F32)
    reps = tq // LANES

    def chunk(kj, masked):
        off = pl.multiple_of(kj * tq, tq)
        kb = kbf[pl.ds(off, tq), :]
        vb = vbf[pl.ds(off, tq), :]
        for c in range(2):
            s = _dot_nt(qc[c], kb)
            if masked:
                row = lax.broadcasted_iota(jnp.int32, s.shape, 0)
                col = lax.broadcasted_iota(jnp.int32, s.shape, 1)
                s = jnp.where(col <= row, s, NEG_INF)
            _softmax_update(s, vb, *stats[c], reps, 1)

    def body(kj, carry):
        chunk(kj, False)
        return carry

    lax.fori_loop(0, qi, body, 0)
    chunk(qi, True)

    lam = _lambda_value(alam_ref, lam_init)
    o = a0[...] / l0[...] - lam * (a1[...] / l1[...])
    ms = jnp.mean(o * o, axis=1, keepdims=True)
    o_ref[...] = o * lax.rsqrt(ms + EPS) * g_ref[...] * (1.0 - lam_init)


def diff_attention_prompt(qk2, qkv2, sub_gain, alam, *, batch, seq, n_heads, d_head, lam_init, layer):
    T = qk2.shape[0]
    dv = 2 * d_head
    assert dv == LANES
    tq = _pick(seq, (512, 256, 128))
    nq = seq // tq
    kcol = n_heads
    vcol = 2 * n_heads
    kern = functools.partial(_diff_prompt_kernel, tq=tq, d_head=d_head, lam_init=lam_init)
    return pl.pallas_call(
        kern,
        grid=(batch, n_heads, nq),
        in_specs=[pl.BlockSpec((tq, LANES), lambda b, h, i: (b * nq + i, h)),
                  pl.BlockSpec((seq, LANES), lambda b, h, i: (b, kcol + h)),
                  pl.BlockSpec((seq, LANES), lambda b, h, i: (b, vcol + h)),
                  pl.BlockSpec((None, 1, LANES), lambda b, h, i: (layer, 0, 0)),
                  pl.BlockSpec((None, 4, d_head), lambda b, h, i: (layer, 0, 0))],
        out_specs=pl.BlockSpec((tq, LANES), lambda b, h, i: (b * nq + i, h)),
        out_shape=jax.ShapeDtypeStruct((T, n_heads * dv), F32),
        scratch_shapes=[pltpu.VMEM((seq, LANES), BF16), pltpu.VMEM((seq, LANES), BF16)]
        + [pltpu.VMEM((tq, LANES), F32) for _ in range(6)],
        compiler_params=_params(("arbitrary", "arbitrary", "arbitrary")),
        name="diff_attn_prompt",
    )(qk2, qk2, qkv2, sub_gain.reshape(sub_gain.shape[0], 1, dv), alam)


def _diff_sample_kernel(pt_ref, q_ref, kn_ref, vn_ref, *rest, n_steps, pps, n_heads, d_head, nq, lam_init):
    del pt_ref
    kp_refs, vp_refs = rest[:pps], rest[pps:2 * pps]
    g_ref, alam_ref, oin_ref, o_ref, qbd, m, l, acc = rest[2 * pps:]
    del oin_ref
    p = pl.program_id(1)
    dm = n_heads * 2 * d_head
    dv = 2 * d_head
    rph = 2 * nq
    nrow = n_heads * rph

    @pl.when(p == 0)
    def _():
        row = lax.broadcasted_iota(jnp.int32, (nrow, dm), 0)
        col = lax.broadcasted_iota(jnp.int32, (nrow, dm), 1)
        qt = jnp.concatenate([q_ref[...]] * (2 * n_heads), axis=0)
        qbd[...] = jnp.where(col // d_head == row // nq, qt, 0.0).astype(BF16)
        m[...] = jnp.full(m.shape, NEG_INF, F32)
        l[...] = jnp.zeros(l.shape, F32)
        acc[...] = jnp.zeros(acc.shape, F32)

    def update(s, v_of_head):
        m_prev = m[...]
        m_new = jnp.maximum(m_prev, jnp.max(s, axis=1, keepdims=True))
        alpha = jnp.exp(m_prev - m_new)
        pr = jnp.exp(s - m_new)
        l[...] = alpha * l[...] + jnp.sum(pr, axis=1, keepdims=True)
        pb = pr.astype(BF16)
        pv = jnp.concatenate([_dot(pb[h * rph:(h + 1) * rph], v_of_head(h)) for h in range(n_heads)], axis=0)
        acc[...] = alpha * acc[...] + pv
        m[...] = m_new

    for i in range(pps):
        s = _dot(qbd[...], kp_refs[i][...].astype(BF16))
        update(s, lambda h, i=i: vp_refs[i][pl.ds(h, LANES, stride=n_heads), :].astype(BF16))

    @pl.when(p == n_steps - 1)
    def _():
        kn = jnp.concatenate([kn_ref[...], jnp.zeros((LANES - nq, dm), F32)], axis=0).astype(BF16)
        r = lax.broadcasted_iota(jnp.int32, (nrow, LANES), 0)
        j = lax.broadcasted_iota(jnp.int32, (nrow, LANES), 1)
        s = jnp.where(j <= r % nq, _dot_nt(qbd[...], kn), NEG_INF)
        vpad = jnp.zeros((LANES - nq, dv), F32)
        update(s, lambda h: jnp.concatenate([vn_ref[:, h * dv:(h + 1) * dv], vpad], axis=0).astype(BF16))
        lam = _lambda_value(alam_ref, lam_init)
        on = acc[...] / l[...]
        for h in range(n_heads):
            oh = on[h * rph:h * rph + nq] - lam * on[h * rph + nq:(h + 1) * rph]
            ms = jnp.mean(oh * oh, axis=1, keepdims=True)
            o_ref[:, h * dv:(h + 1) * dv] = oh * lax.rsqrt(ms + EPS) * g_ref[...] * (1.0 - lam_init)


def diff_attention_sample(o_prompt, qk2, qkv2, cache_k, cache_v, page_table, sub_gain, alam, *,
                          n_prompt_rows, n_heads, d_head, nq, lam_init, layer):
    T, dm = o_prompt.shape
    nseq, n_pages = page_table.shape
    n_layers, n_pool, page = cache_k.shape[:3]
    dv = 2 * d_head
    assert page == LANES and 2 * n_heads * nq == LANES and nq == GROUP and dv == LANES
    ck = jnp.transpose(cache_k, (0, 1, 3, 4, 5, 2)).reshape(n_layers, n_pool, dm, page)
    cv = cache_v.reshape(n_layers, n_pool, page * n_heads, dv)
    r0 = n_prompt_rows // nq
    pps = _pick(n_pages, (4, 2, 1))
    n_steps = n_pages // pps

    def page_map(i):
        return lambda b, p, pt: (layer, pt[b * n_pages + p * pps + i], 0, 0)

    in_specs = [pl.BlockSpec((nq, dm), lambda b, p, pt: (r0 + b, 0)),
                pl.BlockSpec((nq, dm), lambda b, p, pt: (r0 + b, 1)),
                pl.BlockSpec((nq, dm), lambda b, p, pt: (r0 + b, 2))]
    in_specs += [pl.BlockSpec((None, None, dm, page), page_map(i)) for i in range(pps)]
    in_specs += [pl.BlockSpec((None, None, page * n_heads, dv), page_map(i)) for i in range(pps)]
    in_specs += [pl.BlockSpec((None, 1, dv), lambda b, p, pt: (layer, 0, 0)),
                 pl.BlockSpec((None, 4, d_head), lambda b, p, pt: (layer, 0, 0)),
                 pl.BlockSpec(memory_space=pl.ANY)]
    grid_spec = pltpu.PrefetchScalarGridSpec(
        num_scalar_prefetch=1,
        grid=(nseq, n_steps),
        in_specs=in_specs,
        out_specs=pl.BlockSpec((nq, dm), lambda b, p, pt: (r0 + b, 0)),
        scratch_shapes=[pltpu.VMEM((LANES, dm), BF16), pltpu.VMEM((LANES, LANES), F32),
                        pltpu.VMEM((LANES, LANES), F32), pltpu.VMEM((LANES, dv), F32)],
    )
    kern = functools.partial(_diff_sample_kernel, n_steps=n_steps, pps=pps, n_heads=n_heads, d_head=d_head,
                             nq=nq, lam_init=lam_init)
    return pl.pallas_call(
        kern,
        grid_spec=grid_spec,
        out_shape=jax.ShapeDtypeStruct((T, dm), F32),
        input_output_aliases={6 + 2 * pps: 0},
        compiler_params=_params(("arbitrary", "arbitrary")),
        name="diff_attn_sample",
    )(page_table.reshape(-1), qk2, qk2, qkv2, *([ck] * pps), *([cv] * pps),
      sub_gain.reshape(sub_gain.shape[0], 1, dv), alam, o_prompt)


def _later_matrix(n):
    j = lax.broadcasted_iota(jnp.int32, (n, n), 0)
    s = lax.broadcasted_iota(jnp.int32, (n, n), 1)
    return jnp.where(j > s, 1.0, 0.0).astype(BF16)


def _stick_block(zn, log_stay, carry, later, mask):
    reps = zn.shape[1] // LANES
    hi, lo = _split2(log_stay)
    c_b = carry if reps == 1 else jnp.concatenate([carry] * reps, axis=1)
    log_later = _dot(hi, later) + _dot(lo, later) + c_b
    w = jnp.exp(log_stay - zn + log_later)
    if mask is not None:
        w = jnp.where(mask, w, 0.0)
    return w, carry + jnp.sum(log_stay, axis=1, keepdims=True)


def _log_stay(zn, mask):
    ls = jnp.minimum(zn, 0.0) - jnp.log(1.0 + jnp.exp(-jnp.abs(zn)))
    return ls if mask is None else jnp.where(mask, ls, 0.0)


def _sb_prompt_kernel(q_ref, k_ref, v_ref, o_ref, kbf, vbf, carry0, acc0, carry1, acc1, *, tq, cb, d_head, scale):
    qi = pl.program_id(2)

    @pl.when(qi == 0)
    def _():
        kbf[...] = k_ref[...].astype(BF16)
        vbf[...] = v_ref[...].astype(BF16)

    q = q_ref[...] * (-scale)
    lane = lax.broadcasted_iota(jnp.int32, q.shape, 1)
    first = lane < d_head
    qh = (jnp.where(first, q, 0.0).astype(BF16), jnp.where(first, 0.0, q).astype(BF16))
    later = _later_matrix(cb)
    state = ((carry0, acc0), (carry1, acc1))
    for c_ref, a_ref in state:
        c_ref[...] = jnp.zeros(c_ref.shape, F32)
        a_ref[...] = jnp.zeros(a_ref.shape, F32)

    def chunk(kj, masked):
        off = pl.multiple_of(kj * tq, tq)
        kb = kbf[pl.ds(off, tq), :]
        vb = vbf[pl.ds(off, tq), :]
        mask = None
        if masked:
            row = lax.broadcasted_iota(jnp.int32, (tq, tq), 0)
            col = lax.broadcasted_iota(jnp.int32, (tq, tq), 1)
            mask = col < row
        for h in range(2):
            c_ref, a_ref = state[h]
            zn = _dot_nt(qh[h], kb)
            ls = _log_stay(zn, mask)
            c = c_ref[...]
            acc = a_ref[...]
            for b in reversed(range(tq // cb)):
                sl = slice(b * cb, (b + 1) * cb)
                w, c = _stick_block(zn[:, sl], ls[:, sl], c, later, None if mask is None else mask[:, sl])
                acc = acc + _dot(w.astype(BF16), vb[sl, :])
            c_ref[...] = c
            a_ref[...] = acc

    chunk(qi, True)

    def body(t, c):
        chunk(qi - 1 - t, False)
        return c

    lax.fori_loop(0, qi, body, 0)
    o_ref[...] = jnp.where(first, acc0[...], acc1[...])


def sb_attention_prompt(qkv2, *, batch, seq, n_heads, d_head):
    T = qkv2.shape[0]
    dm = n_heads * d_head
    tq = _pick(seq, (512, 256, 128))
    cb = min(256, tq)
    nq = seq // tq
    npair = dm // LANES
    kern = functools.partial(_sb_prompt_kernel, tq=tq, cb=cb, d_head=d_head, scale=d_head ** -0.5)
    return pl.pallas_call(
        kern,
        grid=(batch, npair, nq),
        in_specs=[pl.BlockSpec((tq, LANES), lambda b, h, i: (b * nq + i, h)),
                  pl.BlockSpec((seq, LANES), lambda b, h, i: (b, npair + h)),
                  pl.BlockSpec((seq, LANES), lambda b, h, i: (b, 2 * npair + h))],
        out_specs=pl.BlockSpec((tq, LANES), lambda b, h, i: (b * nq + i, h)),
        out_shape=jax.ShapeDtypeStruct((T, dm), F32),
        scratch_shapes=[pltpu.VMEM((seq, LANES), BF16), pltpu.VMEM((seq, LANES), BF16)]
        + [pltpu.VMEM((tq, LANES), F32) for _ in range(4)],
        compiler_params=_params(("arbitrary", "arbitrary", "arbitrary")),
        name="sb_attn_prompt",
    )(qkv2, qkv2, qkv2)


def _sb_sample_kernel(pt_ref, q_ref, kn_ref, vn_ref, *rest, n_steps, pps, n_heads, d_head, nq, scale):
    del pt_ref
    kp_refs, vp_refs = rest[:pps], rest[pps:2 * pps]
    oin_ref, o_ref, qbd, carry, acc = rest[2 * pps:]
    del oin_ref
    p = pl.program_id(1)
    dm = n_heads * d_head
    nrow = n_heads * nq
    later = _later_matrix(LANES)

    @pl.when(p == 0)
    def _():
        row = lax.broadcasted_iota(jnp.int32, (nrow, dm), 0)
        col = lax.broadcasted_iota(jnp.int32, (nrow, dm), 1)
        qt = jnp.concatenate([q_ref[...] * (-scale)] * n_heads, axis=0)
        qbd[...] = jnp.where(col // d_head == row // nq, qt, 0.0).astype(BF16)
        pad = jnp.zeros((LANES - nq, dm), F32)
        kn = jnp.concatenate([kn_ref[...], pad], axis=0).astype(BF16)
        vn = jnp.concatenate([vn_ref[...], pad], axis=0).astype(BF16)
        r = lax.broadcasted_iota(jnp.int32, (nrow, LANES), 0)
        j = lax.broadcasted_iota(jnp.int32, (nrow, LANES), 1)
        mask = j < r % nq
        zn = _dot_nt(qbd[...], kn)
        w, c = _stick_block(zn, _log_stay(zn, mask), jnp.zeros((nrow, LANES), F32), later, mask)
        acc[...] = _dot(w.astype(BF16), vn)
        carry[...] = c

    for i in range(pps):
        zn = _dot(qbd[...], kp_refs[i][...].astype(BF16))
        w, c = _stick_block(zn, _log_stay(zn, None), carry[...], later, None)
        acc[...] += _dot_nt(w.astype(BF16), vp_refs[i][...].astype(BF16))
        carry[...] = c

    @pl.when(p == n_steps - 1)
    def _():
        row = lax.broadcasted_iota(jnp.int32, (nrow, dm), 0)
        col = lax.broadcasted_iota(jnp.int32, (nrow, dm), 1)
        comb = jnp.where(col // d_head == row // nq, acc[...], 0.0)
        o = comb[0:nq]
        for h in range(1, n_heads):
            o = o + comb[h * nq:(h + 1) * nq]
        o_ref[...] = o


def sb_attention_sample(o_prompt, qkv2, cache_k, cache_v, page_table, *, n_prompt_rows, n_heads, d_head, nq, layer):
    T, dm = o_prompt.shape
    nseq, n_pages = page_table.shape
    n_layers, n_pool, page = cache_k.shape[:3]
    assert page == LANES and n_heads * nq == LANES and nq == GROUP
    ck = jnp.transpose(cache_k, (0, 1, 3, 4, 2)).reshape(n_layers, n_pool, dm, page)
    cv = jnp.transpose(cache_v, (0, 1, 3, 4, 2)).reshape(n_layers, n_pool, dm, page)
    r0 = n_prompt_rows // nq
    pps = _pick(n_pages, (4, 2, 1))
    n_steps = n_pages // pps

    def page_map(i):
        return lambda b, p, pt: (layer, pt[b * n_pages + n_pages - 1 - (p * pps + i)], 0, 0)

    in_specs = [pl.BlockSpec((nq, dm), lambda b, p, pt: (r0 + b, 0)),
                pl.BlockSpec((nq, dm), lambda b, p, pt: (r0 + b, 1)),
                pl.BlockSpec((nq, dm), lambda b, p, pt: (r0 + b, 2))]
    in_specs += [pl.BlockSpec((None, None, dm, page), page_map(i)) for i in range(pps)] * 2
    in_specs += [pl.BlockSpec(memory_space=pl.ANY)]
    grid_spec = pltpu.PrefetchScalarGridSpec(
        num_scalar_prefetch=1,
        grid=(nseq, n_steps),
        in_specs=in_specs,
        out_specs=pl.BlockSpec((nq, dm), lambda b, p, pt: (r0 + b, 0)),
        scratch_shapes=[pltpu.VMEM((LANES, dm), BF16), pltpu.VMEM((LANES, LANES), F32),
                        pltpu.VMEM((LANES, dm), F32)],
    )
    kern = functools.partial(_sb_sample_kernel, n_steps=n_steps, pps=pps, n_heads=n_heads, d_head=d_head,
                             nq=nq, scale=d_head ** -0.5)
    return pl.pallas_call(
        kern,
        grid_spec=grid_spec,
        out_shape=jax.ShapeDtypeStruct((T, dm), F32),
        input_output_aliases={4 + 2 * pps: 0},
        compiler_params=_params(("arbitrary", "arbitrary")),
        name="sb_attn_sample",
    )(page_table.reshape(-1), qkv2, qkv2, qkv2, *([ck] * pps), *([cv] * pps), o_prompt)


def _gelu_tanh(x):
    return 0.5 * x * (1.0 + jnp.tanh(math.sqrt(2.0 / math.pi) * (x + 0.044715 * x * x * x)))


def _s5_kernel(u_ref, h0r_ref, h0i_ref, wb_ref, ar_ref, ai_ref, wc_ref, d_ref,
               y_ref, hr_ref, hi_ref,
               bur, bui, str_, sti, wbhi, wblo, wcbf, *, nb, tt, n_chunk, slabs_per_step, passes):
    t_idx = pl.program_id(1)
    rows = nb * tt
    dm = u_ref.shape[-1]
    ck = dm // n_chunk
    n_slab = bur.shape[0]
    spc = n_slab // n_chunk
    sk = spc * LANES

    @pl.when((pl.program_id(0) == 0) & (t_idx == 0))
    def _():
        wb = wb_ref[...]
        hi = wb.astype(BF16)
        wbhi[...] = hi
        wblo[...] = (wb - hi.astype(F32)).astype(BF16)
        wcbf[...] = wc_ref[...].astype(BF16)

    @pl.when(t_idx == 0)
    def _():
        str_[...] = h0r_ref[...]
        sti[...] = h0i_ref[...]

    u2 = u_ref[...].reshape(rows, dm)
    for j in range(n_chunk):
        uj = u2[:, j * ck:(j + 1) * ck]
        uh, ul = _split2(uj)
        bu = _dot(uh, wbhi[j])
        if passes == 3:
            bu = bu + _dot(ul, wbhi[j]) + _dot(uh, wblo[j])
        for q in range(spc):
            bur[j * spc + q] = bu[:, q * LANES:(q + 1) * LANES]
            bui[j * spc + q] = bu[:, sk + q * LANES:sk + (q + 1) * LANES]

    rc = min(nb, GROUP)
    for r0 in range(0, nb, rc):
        for s0 in range(0, n_slab, slabs_per_step):
            sl = [slice((s0 + q) * LANES, (s0 + q + 1) * LANES) for q in range(slabs_per_step)]
            ar = [ar_ref[:, c] for c in sl]
            ai = [ai_ref[:, c] for c in sl]
            h0 = tuple(str_[r0:r0 + rc, c] for c in sl) + tuple(sti[r0:r0 + rc, c] for c in sl)

            def step(t, h, r0=r0, s0=s0, ar=ar, ai=ai):
                rows_t = pl.ds(r0 * tt + t, rc, stride=tt)
                out_r, out_i = [], []
                for q in range(slabs_per_step):
                    h_r, h_i = h[q], h[slabs_per_step + q]
                    n_r = ar[q] * h_r - ai[q] * h_i + bur[s0 + q, rows_t, :]
                    n_i = ar[q] * h_i + ai[q] * h_r + bui[s0 + q, rows_t, :]
                    bur[s0 + q, rows_t, :] = n_r
                    bui[s0 + q, rows_t, :] = n_i
                    out_r.append(n_r)
                    out_i.append(n_i)
                return tuple(out_r) + tuple(out_i)

            h = lax.fori_loop(0, tt, step, h0)
            for q, c in enumerate(sl):
                str_[r0:r0 + rc, c] = h[q]
                sti[r0:r0 + rc, c] = h[slabs_per_step + q]

    hr_ref[...] = str_[...]
    hi_ref[...] = sti[...]

    for j in range(n_chunk):
        hcat = jnp.concatenate([bur[j * spc + q] for q in range(spc)] + [bui[j * spc + q] for q in range(spc)], axis=1)
        yj = _dot(hcat.astype(BF16), wcbf[j]) + d_ref[:, j * ck:(j + 1) * ck] * u2[:, j * ck:(j + 1) * ck]
        y_ref[:, :, j * ck:(j + 1) * ck] = _gelu_tanh(yj).reshape(nb, tt, ck)


def s5_scan(u3, h0r, h0i, wb, ar, ai, wc, d_skip, *, nb, tt, passes, name):
    B, T, D = u3.shape
    S = h0r.shape[1]
    n_chunk = wb.shape[0]
    kern = functools.partial(_s5_kernel, nb=nb, tt=tt, n_chunk=n_chunk, slabs_per_step=8, passes=passes)
    full = lambda a: pl.BlockSpec(a.shape, lambda b, t: (0,) * a.ndim)
    return pl.pallas_call(
        kern,
        grid=(B // nb, T // tt),
        in_specs=[pl.BlockSpec((nb, tt, D), lambda b, t: (b, t, 0)),
                  pl.BlockSpec((nb, S), lambda b, t: (b, 0)),
                  pl.BlockSpec((nb, S), lambda b, t: (b, 0)),
                  full(wb), full(ar), full(ai), full(wc), full(d_skip)],
        out_specs=[pl.BlockSpec((nb, tt, D), lambda b, t: (b, t, 0)),
                   pl.BlockSpec((nb, S), lambda b, t: (b, 0)),
                   pl.BlockSpec((nb, S), lambda b, t: (b, 0))],
        out_shape=[jax.ShapeDtypeStruct((B, T, D), F32),
                   jax.ShapeDtypeStruct((B, S), F32),
                   jax.ShapeDtypeStruct((B, S), F32)],
        scratch_shapes=[pltpu.VMEM((S // LANES, nb * tt, LANES), F32), pltpu.VMEM((S // LANES, nb * tt, LANES), F32),
                        pltpu.VMEM((nb, S), F32), pltpu.VMEM((nb, S), F32),
                        pltpu.VMEM(wb.shape, BF16), pltpu.VMEM(wb.shape, BF16), pltpu.VMEM(wc.shape, BF16)],
        compiler_params=_params(("arbitrary", "arbitrary")),
        name=name,
    )(u3, h0r, h0i, wb, ar, ai, wc, d_skip)


def s5_tables(log_dt, a_re, a_im, b_re, b_im, c_re, c_im, n_chunk):
    ng, p = a_re.shape
    ch = b_re.shape[-1]
    dt = jnp.exp(log_dt)[:, None]
    mag, ang = jnp.exp(dt * a_re), dt * a_im
    abr, abi = mag * jnp.cos(ang), mag * jnp.sin(ang)
    den = a_re * a_re + a_im * a_im
    fr = ((abr - 1.0) * a_re + abi * a_im) / den
    fi = (abi * a_re - (abr - 1.0) * a_im) / den
    bbr = fr[..., None] * b_re - fi[..., None] * b_im
    bbi = fr[..., None] * b_im + fi[..., None] * b_re
    gpc = ng // n_chunk
    eye = jnp.eye(gpc, dtype=F32)

    def in_proj(bb):
        b4 = bb.reshape(n_chunk, gpc, p, ch)
        return jnp.einsum("jgpc,gh->jgchp", b4, eye).reshape(n_chunk, gpc * ch, gpc * p)

    def out_proj(cc):
        c4 = cc.reshape(n_chunk, gpc, ch, p)
        return jnp.einsum("jgcp,gh->jgphc", c4, eye).reshape(n_chunk, gpc * p, gpc * ch)

    wb = jnp.concatenate([in_proj(bbr), in_proj(bbi)], axis=2)
    wc = jnp.concatenate([out_proj(c_re), -out_proj(c_im)], axis=1)
    return wb, abr.reshape(1, ng * p), abi.reshape(1, ng * p), wc


def _router_kernel(x_ref, g_ref, sh_ref, sc_ref, w_ref, b_ref, h_ref, gate_ref, idx_ref, rank_ref, cnt_ref,
                   whi, wlo, count, *, bb, rows):
    @pl.when(pl.program_id(0) == 0)
    def _():
        w = w_ref[...]
        hi = w.astype(BF16)
        whi[...] = hi
        wlo[...] = (w - hi.astype(F32)).astype(BF16)
        count[...] = jnp.zeros(count.shape, F32)

    x = x_ref[...]
    ms = jnp.mean(x * x, axis=-1, keepdims=True)
    x = x * lax.rsqrt(ms + EPS) * g_ref[...]
    x = x * (1.0 + sc_ref[...]) + sh_ref[...]
    h_ref[...] = x
    tm = bb * rows
    x2 = x.reshape(tm, x.shape[-1])
    xh, xl = _split2(x2)
    logits = _dot(xh, whi[...]) + _dot(xl, whi[...]) + _dot(xh, wlo[...]) + b_ref[...]
    ne = logits.shape[1]
    lane = lax.broadcasted_iota(jnp.int32, logits.shape, 1).astype(F32)
    kcol = lax.broadcasted_iota(jnp.int32, (tm, TOP_K), 1)
    vals = jnp.zeros((tm, TOP_K), F32)
    idxs = jnp.zeros((tm, TOP_K), F32)
    cur = logits
    hot = []
    for k in range(TOP_K):
        mx = jnp.max(cur, axis=1, keepdims=True)
        am = jnp.min(jnp.where(cur == mx, lane, float(ne)), axis=1, keepdims=True)
        vals = jnp.where(kcol == k, mx, vals)
        idxs = jnp.where(kcol == k, am, idxs)
        sel = lane == am
        hot.append(jnp.where(sel, 1.0, 0.0))
        cur = jnp.where(sel, -jnp.inf, cur)
    e = jnp.exp(vals - vals[:, 0:1])
    gate_ref[...] = e / jnp.sum(e, axis=1, keepdims=True)
    idx_ref[...] = idxs.astype(jnp.int32)
    cnt = hot[0] + hot[1] + hot[2] + hot[3]
    r = lax.broadcasted_iota(jnp.int32, (tm, tm), 0)
    c = lax.broadcasted_iota(jnp.int32, (tm, tm), 1)
    before = jnp.where(c < r, 1.0, 0.0).astype(BF16)
    base = _dot(before, cnt.astype(BF16)) + count[...]
    ranks = jnp.zeros((tm, TOP_K), F32)
    for k in range(TOP_K):
        ranks = jnp.where(kcol == k, jnp.sum(hot[k] * base, axis=1, keepdims=True), ranks)
    rank_ref[...] = ranks.astype(jnp.int32)
    count[...] += jnp.sum(cnt, axis=0, keepdims=True)
    cnt_ref[...] = count[...]


def moe_router(x3, gain, layer, modexp, w_router, b_router):
    G, R, K = x3.shape
    E = w_router.shape[-1]
    bb = _pick(G, (64, 32, 16, 8, 4, 2, 1))
    tm = bb * R
    T = G * R
    kern = functools.partial(_router_kernel, bb=bb, rows=R)
    return pl.pallas_call(
        kern,
        grid=(G // bb,),
        in_specs=[pl.BlockSpec((bb, R, K), lambda g: (g, 0, 0)),
                  pl.BlockSpec((None, 1, K), lambda g: (layer, 0, 0)),
                  pl.BlockSpec((None, bb, 1, K), lambda g: (3, g, 0, 0)),
                  pl.BlockSpec((None, bb, 1, K), lambda g: (4, g, 0, 0)),
                  pl.BlockSpec((None, K, E), lambda g: (layer, 0, 0)),
                  pl.BlockSpec((None, 1, E), lambda g: (layer, 0, 0))],
        out_specs=[pl.BlockSpec((bb, R, K), lambda g: (g, 0, 0)),
                   pl.BlockSpec((tm, TOP_K), lambda g: (g, 0)),
                   pl.BlockSpec((tm, TOP_K), lambda g: (g, 0)),
                   pl.BlockSpec((tm, TOP_K), lambda g: (g, 0)),
                   pl.BlockSpec((1, E), lambda g: (0, 0))],
        out_shape=[jax.ShapeDtypeStruct((G, R, K), F32),
                   jax.ShapeDtypeStruct((T, TOP_K), F32),
                   jax.ShapeDtypeStruct((T, TOP_K), jnp.int32),
                   jax.ShapeDtypeStruct((T, TOP_K), jnp.int32),
                   jax.ShapeDtypeStruct((1, E), F32)],
        scratch_shapes=[pltpu.VMEM((K, E), BF16), pltpu.VMEM((K, E), BF16), pltpu.VMEM((1, E), F32)],
        compiler_params=_params(("arbitrary",)),
        name="moe_router",
    )(x3, gain.reshape(gain.shape[0], 1, K), modexp, modexp, w_router,
      b_router.reshape(b_router.shape[0], 1, E))


def _dispatch_kernel(dest_ref, h_ref, xs_in, xs_hbm, sem, *, tm):
    del xs_in
    i = pl.program_id(0)

    def issue(r, c):
        t = i * tm + r
        for k in range(TOP_K):
            pltpu.make_async_copy(h_ref.at[pl.ds(r, 1), :], xs_hbm.at[pl.ds(dest_ref[t * TOP_K + k], 1), :],
                                  sem.at[0]).start()
        return c

    lax.fori_loop(0, tm, issue, 0)
    for k in range(TOP_K):
        pltpu.make_async_copy(h_ref, xs_hbm.at[pl.ds(0, tm), :], sem.at[0]).wait()


def moe_dispatch(h2, dest, n_slots):
    T, D = h2.shape
    tm = _pick(T, (256, 128, 64, 32, 16, 8))
    xs0 = jnp.zeros((n_slots, D), F32)
    grid_spec = pltpu.PrefetchScalarGridSpec(
        num_scalar_prefetch=1,
        grid=(T // tm,),
        in_specs=[pl.BlockSpec((tm, D), lambda i, d: (i, 0)), pl.BlockSpec(memory_space=pl.ANY)],
        out_specs=pl.BlockSpec(memory_space=pl.ANY),
        scratch_shapes=[pltpu.SemaphoreType.DMA((1,))],
    )
    return pl.pallas_call(
        functools.partial(_dispatch_kernel, tm=tm),
        grid_spec=grid_spec,
        out_shape=jax.ShapeDtypeStruct((n_slots, D), F32),
        input_output_aliases={2: 0},
        compiler_params=_params(("arbitrary",)),
        name="moe_dispatch",
    )(dest, h2, xs0)


def _expert_kernel(bexp_ref, nblk_ref, x_ref, wgu_ref, bgu_ref, wd_ref, bd_ref, o_ref, wgu_bf, wd_bf, *, d_ff):
    i = pl.program_id(0)
    n_used = nblk_ref[0]

    @pl.when(i < n_used)
    def _():
        changed = jnp.logical_or(i == 0, bexp_ref[i] != bexp_ref[jnp.maximum(i - 1, 0)])

        @pl.when(changed)
        def _():
            wgu_bf[...] = wgu_ref[...].astype(BF16)
            wd_bf[...] = wd_ref[...].astype(BF16)

        gu = _dot(x_ref[...].astype(BF16), wgu_bf[...]) + bgu_ref[...]
        gate = jnp.minimum(gu[:, :d_ff], SWIGLU_LIMIT)
        up = jnp.clip(gu[:, d_ff:], -SWIGLU_LIMIT, SWIGLU_LIMIT)
        act = (up + 1.0) * gate * _sigmoid(SWIGLU_ALPHA * gate)
        o_ref[...] = _dot(act.astype(BF16), wd_bf[...]) + bd_ref[...]

    @pl.when(i >= n_used)
    def _():
        o_ref[...] = jnp.zeros(o_ref.shape, F32)


def moe_experts(xs, block_expert, n_used, w_gu, b_gu, w_down, b_down, *, layer, blk):
    n_slots, D = xs.shape
    n_blocks = n_slots // blk
    E, _, F2 = w_gu.shape[1:]
    d_ff = F2 // 2
    grid_spec = pltpu.PrefetchScalarGridSpec(
        num_scalar_prefetch=2,
        grid=(n_blocks,),
        in_specs=[pl.BlockSpec((blk, D), lambda i, be, nu: (jnp.minimum(i, nu[0] - 1), 0)),
                  pl.BlockSpec((None, None, D, F2), lambda i, be, nu: (layer, be[i], 0, 0)),
                  pl.BlockSpec((None, None, 1, F2), lambda i, be, nu: (layer, be[i], 0, 0)),
                  pl.BlockSpec((None, None, d_ff, D), lambda i, be, nu: (layer, be[i], 0, 0)),
                  pl.BlockSpec((None, None, 1, D), lambda i, be, nu: (layer, be[i], 0, 0))],
        out_specs=pl.BlockSpec((blk, D), lambda i, be, nu: (i, 0)),
        scratch_shapes=[pltpu.VMEM((D, F2), BF16), pltpu.VMEM((d_ff, D), BF16)],
    )
    return pl.pallas_call(
        functools.partial(_expert_kernel, d_ff=d_ff),
        grid_spec=grid_spec,
        out_shape=jax.ShapeDtypeStruct((n_slots, D), F32),
        compiler_params=_params(("arbitrary",)),
        name="moe_experts",
    )(block_expert, n_used, xs, w_gu, b_gu.reshape(b_gu.shape[0], E, 1, F2), w_down,
      b_down.reshape(b_down.shape[0], E, 1, D))


def _combine_kernel(dest_ref, yb_hbm, gates_ref, res_ref, gate_ref, o_ref, buf, sem, *, tm, bb, rows):
    i = pl.program_id(0)
    n = pl.num_programs(0)
    slot = i % 2

    def gather(tile, s):
        def issue(r, c):
            for k in range(TOP_K):
                src = dest_ref[(tile * tm + r) * TOP_K + k]
                pltpu.make_async_copy(yb_hbm.at[pl.ds(src, 1), :], buf.at[s, k, pl.ds(r, 1), :], sem.at[s]).start()
            return c
        lax.fori_loop(0, tm, issue, 0)

    @pl.when(i == 0)
    def _():
        gather(0, 0)

    @pl.when(i + 1 < n)
    def _():
        gather(i + 1, 1 - slot)

    for k in range(TOP_K):
        pltpu.make_async_copy(yb_hbm.at[pl.ds(0, tm), :], buf.at[slot, k], sem.at[slot]).wait()
    g = gates_ref[...]
    y = g[:, 0:1] * buf[slot, 0]
    for k in range(1, TOP_K):
        y = y + g[:, k:k + 1] * buf[slot, k]
    o_ref[...] = res_ref[...] + gate_ref[...] * y.reshape(bb, rows, y.shape[-1])


def moe_combine(yb, dest, gates, xres3, modexp):
    G, R, D = xres3.shape
    bb = _pick(G, (16, 8, 4, 2, 1))
    tm = bb * R
    grid_spec = pltpu.PrefetchScalarGridSpec(
        num_scalar_prefetch=1,
        grid=(G // bb,),
        in_specs=[pl.BlockSpec(memory_space=pl.ANY),
                  pl.BlockSpec((tm, TOP_K), lambda g, d: (g, 0)),
                  pl.BlockSpec((bb, R, D), lambda g, d: (g, 0, 0)),
                  pl.BlockSpec((None, bb, 1, D), lambda g, d: (5, g, 0, 0))],
        out_specs=pl.BlockSpec((bb, R, D), lambda g, d: (g, 0, 0)),
        scratch_shapes=[pltpu.VMEM((2, TOP_K, tm, D), F32), pltpu.SemaphoreType.DMA((2,))],
    )
    return pl.pallas_call(
        functools.partial(_combine_kernel, tm=tm, bb=bb, rows=R),
        grid_spec=grid_spec,
        out_shape=jax.ShapeDtypeStruct((G, R, D), F32),
        compiler_params=_params(("arbitrary",)),
        name="moe_combine",
    )(dest, yb, gates, xres3, modexp)


def moe_layer(x3, modexp, layer, norm_ffn, w_router, b_router, w_gu, b_gu, w_down, b_down, blk):
    G, R, D = x3.shape
    T = G * R
    E = w_router.shape[-1]
    h3, gates, idx, rank, counts = moe_router(x3, norm_ffn, layer, modexp, w_router, b_router)
    counts = counts.reshape(E).astype(jnp.int32)
    padded = (counts + blk - 1) // blk * blk
    pad_end = jnp.cumsum(padded)
    pad_start = pad_end - padded
    experts = jnp.arange(E, dtype=jnp.int32)
    dest = rank + jnp.sum(jnp.where(idx[..., None] == experts, pad_start, 0), axis=-1)
    dest = dest.reshape(-1).astype(jnp.int32)
    n_blocks = -(-(T * TOP_K) // blk) + E
    block_start = jnp.arange(n_blocks, dtype=jnp.int32) * blk
    block_expert = jnp.minimum(jnp.sum(pad_end[None, :] <= block_start[:, None], axis=1), E - 1).astype(jnp.int32)
    n_used = (pad_end[-1:] // blk).astype(jnp.int32)
    xs = moe_dispatch(h3.reshape(T, D), dest, n_blocks * blk)
    yb = moe_experts(xs, block_expert, n_used, w_gu, b_gu, w_down, b_down, layer=layer, blk=blk)
    return moe_combine(yb, dest, gates, x3, modexp)


def kernel(x_prompt, x_sample, cache_a_k, cache_a_v, state_b_re, state_b_im, cache_c_k, cache_c_v, page_table, c_prompt, c_sample, w_ada, b_ada, norm_mix, norm_ffn, a_w_in, a_q_norm, a_k_norm, a_lambda, a_sub_norm, a_w_out, b_log_dt, b_a_re, b_a_im, b_b_re, b_b_im, b_c_re, b_c_im, b_d, b_w_glu, b_b_glu, c_w_in, c_w_out, moe_w_router, moe_b_router, moe_w_gu, moe_b_gu, moe_w_down, moe_b_down):
    batch, seq, d = x_prompt.shape
    nseq, dec_seq, _ = x_sample.shape
    assert dec_seq == GROUP and seq % GROUP == 0
    depth = w_ada.shape[0]
    n_pages, page = page_table.shape[1], cache_a_k.shape[2]
    past = n_pages * page
    tp, ts = batch * seq, nseq * dec_seq
    T = tp + ts
    G = T // GROUP
    gp = tp // GROUP
    h_a, d_ha = cache_a_k.shape[3], cache_a_k.shape[5]
    h_c, d_hc = cache_c_k.shape[3], cache_c_k.shape[4]
    ssm_g, ssm_p = b_a_re.shape[1], b_a_re.shape[2]
    n_state = ssm_g * ssm_p

    x3 = jnp.concatenate([x_prompt.reshape(gp, GROUP, d), x_sample], axis=0)

    nc = batch + nseq
    ncp = -(-nc // GROUP) * GROUP
    c_all = jnp.concatenate([c_prompt, c_sample, jnp.zeros((ncp - nc, d), F32)], axis=0).reshape(1, ncp, d)
    modexps = []
    for i in range(depth):
        m = fused_matmul(c_all, w_ada, b_ada, layer=i, silu_in=True, passes=3, name="ada")[0]
        m = m.reshape(ncp, 6, d).transpose(1, 0, 2)
        mexp = jnp.concatenate([jnp.repeat(m[:, :batch], seq // GROUP, axis=1), m[:, batch:nc]], axis=1)
        modexps.append(mexp.reshape(6, G, 1, d))

    pos = jnp.concatenate([jnp.tile(jnp.arange(seq, dtype=jnp.int32), batch),
                           jnp.tile(past + jnp.arange(dec_seq, dtype=jnp.int32), nseq)])
    freqs = ROPE_THETA ** (-jnp.arange(0, d_ha, 2, dtype=F32) / d_ha)
    ang = pos.astype(F32)[:, None] * freqs[None, :]
    cos, sin = jnp.cos(ang), jnp.sin(ang)
    reps = LANES // d_ha
    cos_t = jnp.tile(jnp.concatenate([cos, cos], axis=1), (1, reps))
    sin_t = jnp.tile(jnp.concatenate([-sin, sin], axis=1), (1, reps))

    zeros_state = jnp.zeros((batch, n_state), F32)
    moe_blk = 256 if T * TOP_K >= 32768 else 128
    new_a, new_b, new_c = [], [], []
    for i in range(depth):
        kind, j = i % N_MIXERS, i // N_MIXERS
        mexp = modexps[i]
        if kind == 0:
            lam_init = 0.8 - 0.6 * math.exp(-0.3 * i)
            nqk = h_a * 2 * d_ha
            qkv3 = fused_matmul(x3, a_w_in, layer=j, norm=(norm_mix, i, mexp, 0, 1), name="a_qkv")
            qkv2 = qkv3.reshape(T, 3 * nqk)
            gains = jnp.stack([jnp.tile(a_q_norm[j], reps) * (d_ha ** -0.5), jnp.tile(a_k_norm[j], reps)])
            qk2 = qknorm_rope(qkv2, gains.reshape(2, 1, LANES), cos_t, sin_t, 2 * nqk, d_ha)
            o2 = diff_attention_prompt(qk2, qkv2, a_sub_norm, a_lambda, batch=batch, seq=seq, n_heads=h_a,
                                       d_head=d_ha, lam_init=lam_init, layer=j)
            o2 = diff_attention_sample(o2, qk2, qkv2, cache_a_k, cache_a_v, page_table, a_sub_norm, a_lambda,
                                       n_prompt_rows=tp, n_heads=h_a, d_head=d_ha, nq=dec_seq,
                                       lam_init=lam_init, layer=j)
            x3 = fused_matmul(o2.reshape(G, GROUP, nqk), a_w_out, layer=j, resid=(x3, mexp, 2), name="a_out")
            k2, v2 = qk2[:, nqk:], qkv2[:, 2 * nqk:]
            new_a.append((k2[:tp].reshape(batch, seq, h_a, 2, d_ha), v2[:tp].reshape(batch, seq, h_a, 2 * d_ha),
                          k2[tp:].reshape(nseq, dec_seq, h_a, 2, d_ha), v2[tp:].reshape(nseq, dec_seq, h_a, 2 * d_ha)))
        elif kind == 1:
            u3 = mod_norm(x3, norm_mix, i, mexp, 0, 1)
            wb, ar, ai, wc = s5_tables(b_log_dt[j], b_a_re[j], b_a_im[j], b_b_re[j], b_b_im[j],
                                       b_c_re[j], b_c_im[j], ssm_g // GROUP)
            d_skip = b_d[j].reshape(1, d)
            y_p, hpr, hpi = s5_scan(u3[:gp].reshape(batch, seq, d), zeros_state, zeros_state, wb, ar, ai, wc,
                                    d_skip, nb=batch, tt=_pick(seq, (64, 32, 16, 8)), passes=1, name="s5_prompt")
            y_s, hsr, hsi = s5_scan(u3[gp:], state_b_re[j].reshape(nseq, n_state), state_b_im[j].reshape(nseq, n_state),
                                    wb, ar, ai, wc, d_skip, nb=_pick(nseq, (32, 16, 8)), tt=dec_seq, passes=3,
                                    name="s5_sample")
            y3 = jnp.concatenate([y_p.reshape(gp, GROUP, d), y_s], axis=0)
            x3 = fused_matmul(y3, b_w_glu, b_b_glu, layer=j, glu=True, resid=(x3, mexp, 2), name="b_glu")
            st = lambda a, n: a.reshape(n, ssm_g, ssm_p)
            new_b.append((st(hpr, batch), st(hpi, batch), st(hsr, nseq), st(hsi, nseq)))
        else:
            dm = h_c * d_hc
            qkv3 = fused_matmul(x3, c_w_in, layer=j, norm=(norm_mix, i, mexp, 0, 1), name="c_qkv")
            qkv2 = qkv3.reshape(T, 3 * dm)
            o2 = sb_attention_prompt(qkv2, batch=batch, seq=seq, n_heads=h_c, d_head=d_hc)
            o2 = sb_attention_sample(o2, qkv2, cache_c_k, cache_c_v, page_table, n_prompt_rows=tp,
                                     n_heads=h_c, d_head=d_hc, nq=dec_seq, layer=j)
            x3 = fused_matmul(o2.reshape(G, GROUP, dm), c_w_out, layer=j, resid=(x3, mexp, 2), name="c_out")
            k2, v2 = qkv2[:, dm:2 * dm], qkv2[:, 2 * dm:]
            new_c.append((k2[:tp].reshape(batch, seq, h_c, d_hc), v2[:tp].reshape(batch, seq, h_c, d_hc),
                          k2[tp:].reshape(nseq, dec_seq, h_c, d_hc), v2[tp:].reshape(nseq, dec_seq, h_c, d_hc)))
        x3 = moe_layer(x3, mexp, i, norm_ffn, moe_w_router, moe_b_router, moe_w_gu, moe_b_gu,
                       moe_w_down, moe_b_down, moe_blk)

    outs = [x3[:gp].reshape(batch, seq, d), x3[gp:]]
    for group in (new_a, new_b, new_c):
        for k in range(4):
            outs.append(jnp.stack([entry[k] for entry in group]))
    return tuple(outs)
```

```python
import functools
import math

import jax
import jax.numpy as jnp
from jax import lax
from jax.experimental import pallas as pl
from jax.experimental.pallas import tpu as pltpu

F32 = jnp.float32
BF16 = jnp.bfloat16

EPS = 1e-6
NEG_INF = -1e30
ROPE_THETA = 10000.0
TOP_K = 4
SWIGLU_LIMIT = 7.0
SWIGLU_ALPHA = 1.702
N_MIXERS = 3
GROUP = 8
LANES = 128
VMEM_LIMIT = 56 * 1024 * 1024


def _params(sem, vmem=VMEM_LIMIT):
    return pltpu.CompilerParams(dimension_semantics=sem, vmem_limit_bytes=vmem)


def _pick(n, cands):
    for c in cands:
        if n % c == 0:
            return c
    raise ValueError(f"no tile for {n} in {cands}")


def _split2(x):
    hi = x.astype(BF16)
    lo = (x - hi.astype(F32)).astype(BF16)
    return hi, lo


def _sigmoid(x):
    return 1.0 / (1.0 + jnp.exp(-x))


def _dot(a, b):
    return jnp.dot(a, b, preferred_element_type=F32)


def _dot_nt(a, b):
    return lax.dot_general(a, b, (((1,), (1,)), ((), ())), preferred_element_type=F32)


def _mm_kernel(*refs, norm, silu_in, glu, has_bias, resid, passes, bb, rows):
    it = iter(refs)
    x_ref = next(it)
    if norm:
        g_ref, sh_ref, sc_ref = next(it), next(it), next(it)
    w_refs = [next(it)] + ([next(it)] if glu else [])
    b_refs = ([next(it)] + ([next(it)] if glu else [])) if has_bias else []
    if resid:
        res_ref, gate_ref = next(it), next(it)
    o_ref = next(it)
    whi = [next(it) for _ in w_refs]
    wlo = [next(it) for _ in w_refs] if passes == 3 else []

    @pl.when(pl.program_id(1) == 0)
    def _():
        for i, w_ref in enumerate(w_refs):
            w = w_ref[...]
            hi = w.astype(BF16)
            whi[i][...] = hi
            if passes == 3:
                wlo[i][...] = (w - hi.astype(F32)).astype(BF16)

    x = x_ref[...]
    if silu_in:
        x = x * _sigmoid(x)
    if norm:
        ms = jnp.mean(x * x, axis=-1, keepdims=True)
        x = x * lax.rsqrt(ms + EPS) * g_ref[...]
        x = x * (1.0 + sc_ref[...]) + sh_ref[...]
    x2 = x.reshape(bb * rows, x.shape[-1])
    xh = x2.astype(BF16)
    xl = (x2 - xh.astype(F32)).astype(BF16) if passes == 3 else None

    def mm(i):
        acc = _dot(xh, whi[i][...])
        if passes == 3:
            acc = acc + _dot(xl, whi[i][...]) + _dot(xh, wlo[i][...])
        if has_bias:
            acc = acc + b_refs[i][...]
        return acc

    z = mm(0)
    if glu:
        z = z * _sigmoid(mm(1))
    z3 = z.reshape(bb, rows, z.shape[-1])
    if resid:
        z3 = res_ref[...] + gate_ref[...] * z3
    o_ref[...] = z3


def fused_matmul(x3, w, b=None, *, layer=0, norm=None, silu_in=False, glu=False,
                 resid=None, passes=1, tn=None, name="mm"):
    G, R, K = x3.shape
    N = w.shape[-1]
    n_out = N // 2 if glu else N
    bb = _pick(G, (64, 32, 16, 8, 4, 2, 1))
    tn = tn or min(n_out, 1024)
    assert n_out % tn == 0
    half = n_out // tn

    in_specs = [pl.BlockSpec((bb, R, K), lambda n, g: (g, 0, 0))]
    args = [x3]
    if norm is not None:
        gain, gl, modexp, i_sh, i_sc = norm
        gain3 = gain.reshape(gain.shape[0], 1, K)
        in_specs += [pl.BlockSpec((None, 1, K), lambda n, g: (gl, 0, 0)),
                     pl.BlockSpec((None, bb, 1, K), lambda n, g: (i_sh, g, 0, 0)),
                     pl.BlockSpec((None, bb, 1, K), lambda n, g: (i_sc, g, 0, 0))]
        args += [gain3, modexp, modexp]
    in_specs.append(pl.BlockSpec((None, K, tn), lambda n, g: (layer, 0, n)))
    args.append(w)
    if glu:
        in_specs.append(pl.BlockSpec((None, K, tn), lambda n, g: (layer, 0, n + half)))
        args.append(w)
    if b is not None:
        b3 = b.reshape(b.shape[0], 1, N)
        in_specs.append(pl.BlockSpec((None, 1, tn), lambda n, g: (layer, 0, n)))
        args.append(b3)
        if glu:
            in_specs.append(pl.BlockSpec((None, 1, tn), lambda n, g: (layer, 0, n + half)))
            args.append(b3)
    if resid is not None:
        xres, modexp_r, i_g = resid
        in_specs += [pl.BlockSpec((bb, R, tn), lambda n, g: (g, 0, n)),
                     pl.BlockSpec((None, bb, 1, tn), lambda n, g: (i_g, g, 0, n))]
        args += [xres, modexp_r]
    nw = 2 if glu else 1
    scratch = [pltpu.VMEM((K, tn), BF16) for _ in range(nw * (2 if passes == 3 else 1))]
    kern = functools.partial(_mm_kernel, norm=norm is not None, silu_in=silu_in, glu=glu,
                             has_bias=b is not None, resid=resid is not None, passes=passes,
                             bb=bb, rows=R)
    return pl.pallas_call(
        kern,
        grid=(n_out // tn, G // bb),
        in_specs=in_specs,
        out_specs=pl.BlockSpec((bb, R, tn), lambda n, g: (g, 0, n)),
        out_shape=jax.ShapeDtypeStruct((G, R, n_out), F32),
        scratch_shapes=scratch,
        compiler_params=_params(("arbitrary", "arbitrary")),
        name=name,
    )(*args)


def _mod_norm_kernel(x_ref, g_ref, sh_ref, sc_ref, o_ref):
    x = x_ref[...]
    ms = jnp.mean(x * x, axis=-1, keepdims=True)
    x = x * lax.rsqrt(ms + EPS) * g_ref[...]
    o_ref[...] = x * (1.0 + sc_ref[...]) + sh_ref[...]


def mod_norm(x3, gain, layer, modexp, i_sh, i_sc):
    G, R, K = x3.shape
    bb = _pick(G, (64, 32, 16, 8, 4, 2, 1))
    return pl.pallas_call(
        _mod_norm_kernel,
        grid=(G // bb,),
        in_specs=[pl.BlockSpec((bb, R, K), lambda g: (g, 0, 0)),
                  pl.BlockSpec((None, 1, K), lambda g: (layer, 0, 0)),
                  pl.BlockSpec((None, bb, 1, K), lambda g: (i_sh, g, 0, 0)),
                  pl.BlockSpec((None, bb, 1, K), lambda g: (i_sc, g, 0, 0))],
        out_specs=pl.BlockSpec((bb, R, K), lambda g: (g, 0, 0)),
        out_shape=jax.ShapeDtypeStruct((G, R, K), F32),
        compiler_params=_params(("parallel",)),
        name="mod_norm",
    )(x3, gain.reshape(gain.shape[0], 1, K), modexp, modexp)


def _qknorm_rope_kernel(x_ref, g_ref, cos_ref, sin_ref, o_ref, *, d_head):
    x = x_ref[...]
    r = lax.broadcasted_iota(jnp.int32, (LANES, LANES), 0) // d_head
    c = lax.broadcasted_iota(jnp.int32, (LANES, LANES), 1) // d_head
    seg = jnp.where(r == c, 1.0, 0.0).astype(BF16)
    hi, lo = _split2(x * x)
    ss = _dot(hi, seg) + _dot(lo, seg)
    y = x * lax.rsqrt(ss * (1.0 / d_head) + EPS) * g_ref[...]
    lane = lax.broadcasted_iota(jnp.int32, x.shape, 1)
    half = d_head // 2
    rot = jnp.where(lane % d_head < half, pltpu.roll(y, LANES - half, axis=1), pltpu.roll(y, half, axis=1))
    o_ref[...] = y * cos_ref[...] + rot * sin_ref[...]


def qknorm_rope(qkv2, gains, cos_t, sin_t, n_qk_cols, d_head):
    T = qkv2.shape[0]
    tm = _pick(T, (1024, 512, 256, 128, 64, 32, 16, 8))
    ncb = n_qk_cols // LANES
    per = ncb // 2
    return pl.pallas_call(
        functools.partial(_qknorm_rope_kernel, d_head=d_head),
        grid=(T // tm, ncb),
        in_specs=[pl.BlockSpec((tm, LANES), lambda i, j: (i, j)),
                  pl.BlockSpec((None, 1, LANES), lambda i, j: (j // per, 0, 0)),
                  pl.BlockSpec((tm, LANES), lambda i, j: (i, 0)),
                  pl.BlockSpec((tm, LANES), lambda i, j: (i, 0))],
        out_specs=pl.BlockSpec((tm, LANES), lambda i, j: (i, j)),
        out_shape=jax.ShapeDtypeStruct((T, n_qk_cols), F32),
        compiler_params=_params(("parallel", "parallel")),
        name="qknorm_rope",
    )(qkv2, gains, cos_t, sin_t)


def _lambda_value(alam_ref, lam_init):
    a = alam_ref[...]
    s1 = jnp.sum(a[0:1] * a[1:2], axis=1, keepdims=True)
    s2 = jnp.sum(a[2:3] * a[3:4], axis=1, keepdims=True)
    return jnp.exp(s1) - jnp.exp(s2) + lam_init


def _softmax_update(s, v_bf, m_ref, l_ref, acc_ref, reps_s, reps_acc):
    m_prev = m_ref[...]
    m_new = jnp.maximum(m_prev, jnp.max(s, axis=1, keepdims=True))
    alpha = jnp.exp(m_prev - m_new)
    m_b = m_new if reps_s == 1 else jnp.concatenate([m_new] * reps_s, axis=1)
    p = jnp.exp(s - m_b)
    l_ref[...] = alpha * l_ref[...] + jnp.sum(p, axis=1, keepdims=True)
    a_b = alpha if reps_acc == 1 else jnp.concatenate([alpha] * reps_acc, axis=1)
    acc_ref[...] = a_b * acc_ref[...] + _dot(p.astype(BF16), v_bf)
    m_ref[...] = m_new


def _diff_prompt_kernel(q_ref, k_ref, v_ref, g_ref, alam_ref, o_ref,
                        kbf, vbf, m0, l0, a0, m1, l1, a1, *, tq, d_head, lam_init):
    qi = pl.program_id(2)

    @pl.when(qi == 0)
    def _():
        kbf[...] = k_ref[...].astype(BF16)
        vbf[...] = v_ref[...].astype(BF16)

    q = q_ref[...]
    lane = lax.broadcasted_iota(jnp.int32, q.shape, 1)
    qc = (jnp.where(lane < d_head, q, 0.0).astype(BF16), jnp.where(lane >= d_head, q, 0.0).astype(BF16))
    stats = ((m0, l0, a0), (m1, l1, a1))
    for m, l, a in stats:
        m[...] = jnp.full(m.shape, NEG_INF, F32)
        l[...] = jnp.zeros(l.shape, F32)
        a[...] = jnp.zeros(a.shape, F32)
    reps = tq // LANES

    def chunk(kj, masked):
        off = pl.multiple_of(kj * tq, tq)
        kb = kbf[pl.ds(off, tq), :]
        vb = vbf[pl.ds(off, tq), :]
        for c in range(2):
            s = _dot_nt(qc[c], kb)
            if masked:
                row = lax.broadcasted_iota(jnp.int32, s.shape, 0)
                col = lax.broadcasted_iota(jnp.int32, s.shape, 1)
                s = jnp.where(col <= row, s, NEG_INF)
            _softmax_update(s, vb, *stats[c], reps, 1)

    def body(kj, carry):
        chunk(kj, False)
        return carry

    lax.fori_loop(0, qi, body, 0)
    chunk(qi, True)

    lam = _lambda_value(alam_ref, lam_init)
    o = a0[...] / l0[...] - lam * (a1[...] / l1[...])
    ms = jnp.mean(o * o, axis=1, keepdims=True)
    o_ref[...] = o * lax.rsqrt(ms + EPS) * g_ref[...] * (1.0 - lam_init)


def diff_attention_prompt(qk2, qkv2, sub_gain, alam, *, batch, seq, n_heads, d_head, lam_init, layer):
    T = qk2.shape[0]
    dv = 2 * d_head
    assert dv == LANES
    tq = _pick(seq, (512, 256, 128))
    nq = seq // tq
    kcol = n_heads
    vcol = 2 * n_heads
    kern = functools.partial(_diff_prompt_kernel, tq=tq, d_head=d_head, lam_init=lam_init)
    return pl.pallas_call(
        kern,
        grid=(batch, n_heads, nq),
        in_specs=[pl.BlockSpec((tq, LANES), lambda b, h, i: (b * nq + i, h)),
                  pl.BlockSpec((seq, LANES), lambda b, h, i: (b, kcol + h)),
                  pl.BlockSpec((seq, LANES), lambda b, h, i: (b, vcol + h)),
                  pl.BlockSpec((None, 1, LANES), lambda b, h, i: (layer, 0, 0)),
                  pl.BlockSpec((None, 4, d_head), lambda b, h, i: (layer, 0, 0))],
        out_specs=pl.BlockSpec((tq, LANES), lambda b, h, i: (b * nq + i, h)),
        out_shape=jax.ShapeDtypeStruct((T, n_heads * dv), F32),
        scratch_shapes=[pltpu.VMEM((seq, LANES), BF16), pltpu.VMEM((seq, LANES), BF16)]
        + [pltpu.VMEM((tq, LANES), F32) for _ in range(6)],
        compiler_params=_params(("arbitrary", "arbitrary", "arbitrary")),
        name="diff_attn_prompt",
    )(qk2, qk2, qkv2, sub_gain.reshape(sub_gain.shape[0], 1, dv), alam)


def _diff_sample_kernel(pt_ref, q_ref, kn_ref, vn_ref, *rest, n_steps, pps, n_heads, d_head, nq, lam_init):
    del pt_ref
    kp_refs, vp_refs = rest[:pps], rest[pps:2 * pps]
    g_ref, alam_ref, oin_ref, o_ref, qbd, m, l, acc = rest[2 * pps:]
    del oin_ref
    p = pl.program_id(1)
    dm = n_heads * 2 * d_head
    dv = 2 * d_head
    rph = 2 * nq
    nrow = n_heads * rph

    @pl.when(p == 0)
    def _():
        row = lax.broadcasted_iota(jnp.int32, (nrow, dm), 0)
        col = lax.broadcasted_iota(jnp.int32, (nrow, dm), 1)
        qt = jnp.concatenate([q_ref[...]] * (2 * n_heads), axis=0)
        qbd[...] = jnp.where(col // d_head == row // nq, qt, 0.0).astype(BF16)
        m[...] = jnp.full(m.shape, NEG_INF, F32)
        l[...] = jnp.zeros(l.shape, F32)
        acc[...] = jnp.zeros(acc.shape, F32)

    def update(s, v_of_head):
        reps = s.shape[1] // LANES
        m_prev = m[...]
        m_new = jnp.maximum(m_prev, jnp.max(s, axis=1, keepdims=True))
        alpha = jnp.exp(m_prev - m_new)
        pr = jnp.exp(s - (m_new if reps == 1 else jnp.concatenate([m_new] * reps, axis=1)))
        l[...] = alpha * l[...] + jnp.sum(pr, axis=1, keepdims=True)
        pb = pr.astype(BF16)
        pv = jnp.concatenate([_dot(pb[h * rph:(h + 1) * rph], v_of_head(h)) for h in range(n_heads)], axis=0)
        acc[...] = alpha * acc[...] + pv
        m[...] = m_new

    qb = qbd[...]
    s = jnp.concatenate([_dot(qb, kp_refs[i][...].astype(BF16)) for i in range(pps)], axis=1)
    update(s, lambda h: jnp.concatenate(
        [vp_refs[i][pl.ds(h, LANES, stride=n_heads), :].astype(BF16) for i in range(pps)], axis=0))

    @pl.when(p == n_steps - 1)
    def _():
        kn = jnp.concatenate([kn_ref[...], jnp.zeros((LANES - nq, dm), F32)], axis=0).astype(BF16)
        r = lax.broadcasted_iota(jnp.int32, (nrow, LANES), 0)
        j = lax.broadcasted_iota(jnp.int32, (nrow, LANES), 1)
        s = jnp.where(j <= r % nq, _dot_nt(qbd[...], kn), NEG_INF)
        vpad = jnp.zeros((LANES - nq, dv), F32)
        update(s, lambda h: jnp.concatenate([vn_ref[:, h * dv:(h + 1) * dv], vpad], axis=0).astype(BF16))
        lam = _lambda_value(alam_ref, lam_init)
        on = acc[...] / l[...]
        for h in range(n_heads):
            oh = on[h * rph:h * rph + nq] - lam * on[h * rph + nq:(h + 1) * rph]
            ms = jnp.mean(oh * oh, axis=1, keepdims=True)
            o_ref[:, h * dv:(h + 1) * dv] = oh * lax.rsqrt(ms + EPS) * g_ref[...] * (1.0 - lam_init)


def diff_attention_sample(o_prompt, qk2, qkv2, cache_k, cache_v, page_table, sub_gain, alam, *,
                          n_prompt_rows, n_heads, d_head, nq, lam_init, layer):
    T, dm = o_prompt.shape
    nseq, n_pages = page_table.shape
    n_layers, n_pool, page = cache_k.shape[:3]
    dv = 2 * d_head
    assert page == LANES and 2 * n_heads * nq == LANES and nq == GROUP and dv == LANES
    ck = jnp.transpose(cache_k, (0, 1, 3, 4, 5, 2)).reshape(n_layers, n_pool, dm, page)
    cv = cache_v.reshape(n_layers, n_pool, page * n_heads, dv)
    r0 = n_prompt_rows // nq
    pps = _pick(n_pages, (4, 2, 1))
    n_steps = n_pages // pps

    def page_map(i):
        return lambda b, p, pt: (layer, pt[b * n_pages + p * pps + i], 0, 0)

    in_specs = [pl.BlockSpec((nq, dm), lambda b, p, pt: (r0 + b, 0)),
                pl.BlockSpec((nq, dm), lambda b, p, pt: (r0 + b, 1)),
                pl.BlockSpec((nq, dm), lambda b, p, pt: (r0 + b, 2))]
    in_specs += [pl.BlockSpec((None, None, dm, page), page_map(i)) for i in range(pps)]
    in_specs += [pl.BlockSpec((None, None, page * n_heads, dv), page_map(i)) for i in range(pps)]
    in_specs += [pl.BlockSpec((None, 1, dv), lambda b, p, pt: (layer, 0, 0)),
                 pl.BlockSpec((None, 4, d_head), lambda b, p, pt: (layer, 0, 0)),
                 pl.BlockSpec(memory_space=pl.ANY)]
    grid_spec = pltpu.PrefetchScalarGridSpec(
        num_scalar_prefetch=1,
        grid=(nseq, n_steps),
        in_specs=in_specs,
        out_specs=pl.BlockSpec((nq, dm), lambda b, p, pt: (r0 + b, 0)),
        scratch_shapes=[pltpu.VMEM((LANES, dm), BF16), pltpu.VMEM((LANES, LANES), F32),
                        pltpu.VMEM((LANES, LANES), F32), pltpu.VMEM((LANES, dv), F32)],
    )
    kern = functools.partial(_diff_sample_kernel, n_steps=n_steps, pps=pps, n_heads=n_heads, d_head=d_head,
                             nq=nq, lam_init=lam_init)
    return pl.pallas_call(
        kern,
        grid_spec=grid_spec,
        out_shape=jax.ShapeDtypeStruct((T, dm), F32),
        input_output_aliases={6 + 2 * pps: 0},
        compiler_params=_params(("arbitrary", "arbitrary")),
        name="diff_attn_sample",
    )(page_table.reshape(-1), qk2, qk2, qkv2, *([ck] * pps), *([cv] * pps),
      sub_gain.reshape(sub_gain.shape[0], 1, dv), alam, o_prompt)


def _later_matrix(n):
    j = lax.broadcasted_iota(jnp.int32, (n, n), 0)
    s = lax.broadcasted_iota(jnp.int32, (n, n), 1)
    return jnp.where(j > s, 1.0, 0.0).astype(BF16)


def _stick_block(zn, log_stay, carry, later, mask):
    reps = zn.shape[1] // LANES
    hi, lo = _split2(log_stay)
    c_b = carry if reps == 1 else jnp.concatenate([carry] * reps, axis=1)
    log_later = _dot(hi, later) + _dot(lo, later) + c_b
    w = jnp.exp(log_stay - zn + log_later)
    if mask is not None:
        w = jnp.where(mask, w, 0.0)
    return w, carry + jnp.sum(log_stay, axis=1, keepdims=True)


def _log_stay(zn, mask):
    ls = jnp.minimum(zn, 0.0) - jnp.log(1.0 + jnp.exp(-jnp.abs(zn)))
    return ls if mask is None else jnp.where(mask, ls, 0.0)


def _sb_prompt_kernel(q_ref, k_ref, v_ref, o_ref, kbf, vbf, carry0, acc0, carry1, acc1, *, tq, cb, d_head, scale):
    qi = pl.program_id(2)

    @pl.when(qi == 0)
    def _():
        kbf[...] = k_ref[...].astype(BF16)
        vbf[...] = v_ref[...].astype(BF16)

    q = q_ref[...] * (-scale)
    lane = lax.broadcasted_iota(jnp.int32, q.shape, 1)
    first = lane < d_head
    qh = (jnp.where(first, q, 0.0).astype(BF16), jnp.where(first, 0.0, q).astype(BF16))
    later = _later_matrix(cb)
    state = ((carry0, acc0), (carry1, acc1))
    for c_ref, a_ref in state:
        c_ref[...] = jnp.zeros(c_ref.shape, F32)
        a_ref[...] = jnp.zeros(a_ref.shape, F32)

    def chunk(kj, masked):
        off = pl.multiple_of(kj * tq, tq)
        kb = kbf[pl.ds(off, tq), :]
        vb = vbf[pl.ds(off, tq), :]
        mask = None
        if masked:
            row = lax.broadcasted_iota(jnp.int32, (tq, tq), 0)
            col = lax.broadcasted_iota(jnp.int32, (tq, tq), 1)
            mask = col < row
        for h in range(2):
            c_ref, a_ref = state[h]
            zn = _dot_nt(qh[h], kb)
            ls = _log_stay(zn, mask)
            c = c_ref[...]
            acc = a_ref[...]
            for b in reversed(range(tq // cb)):
                sl = slice(b * cb, (b + 1) * cb)
                w, c = _stick_block(zn[:, sl], ls[:, sl], c, later, None if mask is None else mask[:, sl])
                acc = acc + _dot(w.astype(BF16), vb[sl, :])
            c_ref[...] = c
            a_ref[...] = acc

    chunk(qi, True)

    def body(t, c):
        chunk(qi - 1 - t, False)
        return c

    lax.fori_loop(0, qi, body, 0)
    o_ref[...] = jnp.where(first, acc0[...], acc1[...])


def sb_attention_prompt(qkv2, *, batch, seq, n_heads, d_head):
    T = qkv2.shape[0]
    dm = n_heads * d_head
    tq = _pick(seq, (512, 256, 128))
    cb = min(256, tq)
    nq = seq // tq
    npair = dm // LANES
    kern = functools.partial(_sb_prompt_kernel, tq=tq, cb=cb, d_head=d_head, scale=d_head ** -0.5)
    return pl.pallas_call(
        kern,
        grid=(batch, npair, nq),
        in_specs=[pl.BlockSpec((tq, LANES), lambda b, h, i: (b * nq + i, h)),
                  pl.BlockSpec((seq, LANES), lambda b, h, i: (b, npair + h)),
                  pl.BlockSpec((seq, LANES), lambda b, h, i: (b, 2 * npair + h))],
        out_specs=pl.BlockSpec((tq, LANES), lambda b, h, i: (b * nq + i, h)),
        out_shape=jax.ShapeDtypeStruct((T, dm), F32),
        scratch_shapes=[pltpu.VMEM((seq, LANES), BF16), pltpu.VMEM((seq, LANES), BF16)]
        + [pltpu.VMEM((tq, LANES), F32) for _ in range(4)],
        compiler_params=_params(("arbitrary", "arbitrary", "arbitrary")),
        name="sb_attn_prompt",
    )(qkv2, qkv2, qkv2)


def _sb_sample_kernel(pt_ref, q_ref, kn_ref, vn_ref, *rest, n_steps, pps, n_heads, d_head, nq, scale):
    del pt_ref
    kp_refs, vp_refs = rest[:pps], rest[pps:2 * pps]
    oin_ref, o_ref, qbd, carry, acc = rest[2 * pps:]
    del oin_ref
    p = pl.program_id(1)
    dm = n_heads * d_head
    nrow = n_heads * nq
    later = _later_matrix(LANES)

    @pl.when(p == 0)
    def _():
        row = lax.broadcasted_iota(jnp.int32, (nrow, dm), 0)
        col = lax.broadcasted_iota(jnp.int32, (nrow, dm), 1)
        qt = jnp.concatenate([q_ref[...] * (-scale)] * n_heads, axis=0)
        qbd[...] = jnp.where(col // d_head == row // nq, qt, 0.0).astype(BF16)
        pad = jnp.zeros((LANES - nq, dm), F32)
        kn = jnp.concatenate([kn_ref[...], pad], axis=0).astype(BF16)
        vn = jnp.concatenate([vn_ref[...], pad], axis=0).astype(BF16)
        r = lax.broadcasted_iota(jnp.int32, (nrow, LANES), 0)
        j = lax.broadcasted_iota(jnp.int32, (nrow, LANES), 1)
        mask = j < r % nq
        zn = _dot_nt(qbd[...], kn)
        w, c = _stick_block(zn, _log_stay(zn, mask), jnp.zeros((nrow, LANES), F32), later, mask)
        acc[...] = _dot(w.astype(BF16), vn)
        carry[...] = c

    qb = qbd[...]
    c = carry[...]
    ws = []
    for i in range(pps):
        zn = _dot(qb, kp_refs[i][...].astype(BF16))
        w, c = _stick_block(zn, _log_stay(zn, None), c, later, None)
        ws.append(w.astype(BF16))
    carry[...] = c
    wcat = ws[0] if pps == 1 else jnp.concatenate(ws, axis=1)
    vcat = jnp.concatenate([vp_refs[i][...].astype(BF16) for i in range(pps)], axis=1)
    acc[...] += _dot_nt(wcat, vcat)

    @pl.when(p == n_steps - 1)
    def _():
        row = lax.broadcasted_iota(jnp.int32, (nrow, dm), 0)
        col = lax.broadcasted_iota(jnp.int32, (nrow, dm), 1)
        comb = jnp.where(col // d_head == row // nq, acc[...], 0.0)
        o = comb[0:nq]
        for h in range(1, n_heads):
            o = o + comb[h * nq:(h + 1) * nq]
        o_ref[...] = o


def sb_attention_sample(o_prompt, qkv2, cache_k, cache_v, page_table, *, n_prompt_rows, n_heads, d_head, nq, layer):
    T, dm = o_prompt.shape
    nseq, n_pages = page_table.shape
    n_layers, n_pool, page = cache_k.shape[:3]
    assert page == LANES and n_heads * nq == LANES and nq == GROUP
    ck = jnp.transpose(cache_k, (0, 1, 3, 4, 2)).reshape(n_layers, n_pool, dm, page)
    cv = jnp.transpose(cache_v, (0, 1, 3, 4, 2)).reshape(n_layers, n_pool, dm, page)
    r0 = n_prompt_rows // nq
    pps = _pick(n_pages, (4, 2, 1))
    n_steps = n_pages // pps

    def page_map(i):
        return lambda b, p, pt: (layer, pt[b * n_pages + n_pages - 1 - (p * pps + i)], 0, 0)

    in_specs = [pl.BlockSpec((nq, dm), lambda b, p, pt: (r0 + b, 0)),
                pl.BlockSpec((nq, dm), lambda b, p, pt: (r0 + b, 1)),
                pl.BlockSpec((nq, dm), lambda b, p, pt: (r0 + b, 2))]
    in_specs += [pl.BlockSpec((None, None, dm, page), page_map(i)) for i in range(pps)] * 2
    in_specs += [pl.BlockSpec(memory_space=pl.ANY)]
    grid_spec = pltpu.PrefetchScalarGridSpec(
        num_scalar_prefetch=1,
        grid=(nseq, n_steps),
        in_specs=in_specs,
        out_specs=pl.BlockSpec((nq, dm), lambda b, p, pt: (r0 + b, 0)),
        scratch_shapes=[pltpu.VMEM((LANES, dm), BF16), pltpu.VMEM((LANES, LANES), F32),
                        pltpu.VMEM((LANES, dm), F32)],
    )
    kern = functools.partial(_sb_sample_kernel, n_steps=n_steps, pps=pps, n_heads=n_heads, d_head=d_head,
                             nq=nq, scale=d_head ** -0.5)
    return pl.pallas_call(
        kern,
        grid_spec=grid_spec,
        out_shape=jax.ShapeDtypeStruct((T, dm), F32),
        input_output_aliases={4 + 2 * pps: 0},
        compiler_params=_params(("arbitrary", "arbitrary")),
        name="sb_attn_sample",
    )(page_table.reshape(-1), qkv2, qkv2, qkv2, *([ck] * pps), *([cv] * pps), o_prompt)


def _gelu_tanh(x):
    return 0.5 * x * (1.0 + jnp.tanh(math.sqrt(2.0 / math.pi) * (x + 0.044715 * x * x * x)))


def _s5_kernel(u_ref, h0r_ref, h0i_ref, wb_ref, ar_ref, ai_ref, wc_ref, d_ref,
               y_ref, hr_ref, hi_ref,
               utm, bur, bui, str_, sti, wbhi, wblo, wcbf, *, nb, tt, n_chunk, slabs_per_step, passes):
    t_idx = pl.program_id(1)
    rows = nb * tt
    dm = u_ref.shape[-1]
    ck = dm // n_chunk
    n_slab = bur.shape[0]
    spc = n_slab // n_chunk
    sk = spc * LANES

    @pl.when((pl.program_id(0) == 0) & (t_idx == 0))
    def _():
        wb = wb_ref[...]
        hi = wb.astype(BF16)
        wbhi[...] = hi
        wblo[...] = (wb - hi.astype(F32)).astype(BF16)
        wcbf[...] = wc_ref[...].astype(BF16)

    @pl.when(t_idx == 0)
    def _():
        str_[...] = h0r_ref[...]
        sti[...] = h0i_ref[...]

    assert ck == LANES
    for s in range(nb):
        for j in range(n_chunk):
            utm[j, pl.ds(s, tt, stride=nb), :] = u_ref[s, :, j * ck:(j + 1) * ck]

    for j in range(n_chunk):
        uh, ul = _split2(utm[j])
        bu = _dot(uh, wbhi[j])
        if passes == 3:
            bu = bu + _dot(ul, wbhi[j]) + _dot(uh, wblo[j])
        for q in range(spc):
            bur[j * spc + q] = bu[:, q * LANES:(q + 1) * LANES]
            bui[j * spc + q] = bu[:, sk + q * LANES:sk + (q + 1) * LANES]

    spv = min(nb, GROUP)
    spr = GROUP // spv
    assert spr in (1, 2) and nb % spv == 0
    low = lax.broadcasted_iota(jnp.int32, (GROUP, LANES), 0) < spv
    rep = (lambda x: x) if spr == 1 else (lambda x: jnp.concatenate([x, x], axis=0))
    for g in range(nb // spv):
        for s0 in range(0, n_slab, slabs_per_step):
            sl = [slice((s0 + q) * LANES, (s0 + q + 1) * LANES) for q in range(slabs_per_step)]
            ar = [jnp.broadcast_to(ar_ref[:, c], (GROUP, LANES)) for c in sl]
            ai = [jnp.broadcast_to(ai_ref[:, c], (GROUP, LANES)) for c in sl]
            h0 = (tuple(rep(str_[g * spv:(g + 1) * spv, c]) for c in sl)
                  + tuple(rep(sti[g * spv:(g + 1) * spv, c]) for c in sl))

            def step(tv, h, g=g, s0=s0, ar=ar, ai=ai):
                rows_t = pl.ds(pl.multiple_of(tv * (spr * nb) + g * GROUP, GROUP), GROUP)
                out_r, out_i = [], []
                for q in range(slabs_per_step):
                    h_r, h_i = h[q], h[slabs_per_step + q]
                    b_r, b_i = bur[s0 + q, rows_t, :], bui[s0 + q, rows_t, :]
                    n_r = ar[q] * h_r - ai[q] * h_i + b_r
                    n_i = ar[q] * h_i + ai[q] * h_r + b_i
                    if spr == 2:
                        p_r, p_i = pltpu.roll(n_r, spv, axis=0), pltpu.roll(n_i, spv, axis=0)
                        m_r = ar[q] * p_r - ai[q] * p_i + b_r
                        m_i = ar[q] * p_i + ai[q] * p_r + b_i
                        n_r, n_i = jnp.where(low, n_r, m_r), jnp.where(low, n_i, m_i)
                        h_r = jnp.where(low, pltpu.roll(m_r, spv, axis=0), m_r)
                        h_i = jnp.where(low, pltpu.roll(m_i, spv, axis=0), m_i)
                    else:
                        h_r, h_i = n_r, n_i
                    bur[s0 + q, rows_t, :] = n_r
                    bui[s0 + q, rows_t, :] = n_i
                    out_r.append(h_r)
                    out_i.append(h_i)
                return tuple(out_r) + tuple(out_i)

            h = lax.fori_loop(0, tt // spr, step, h0)
            for q, c in enumerate(sl):
                str_[g * spv:(g + 1) * spv, c] = h[q][0:spv]
                sti[g * spv:(g + 1) * spv, c] = h[slabs_per_step + q][0:spv]

    hr_ref[...] = str_[...]
    hi_ref[...] = sti[...]

    for j in range(n_chunk):
        hcat = jnp.concatenate([bur[j * spc + q] for q in range(spc)] + [bui[j * spc + q] for q in range(spc)], axis=1)
        yj = _dot(hcat.astype(BF16), wcbf[j]) + d_ref[:, j * ck:(j + 1) * ck] * utm[j]
        utm[j] = _gelu_tanh(yj)
    for s in range(nb):
        for j in range(n_chunk):
            y_ref[s, :, j * ck:(j + 1) * ck] = utm[j, pl.ds(s, tt, stride=nb), :]


def s5_scan(u3, h0r, h0i, wb, ar, ai, wc, d_skip, *, nb, tt, passes, name):
    B, T, D = u3.shape
    S = h0r.shape[1]
    n_chunk = wb.shape[0]
    kern = functools.partial(_s5_kernel, nb=nb, tt=tt, n_chunk=n_chunk, slabs_per_step=8, passes=passes)
    full = lambda a: pl.BlockSpec(a.shape, lambda b, t: (0,) * a.ndim)
    return pl.pallas_call(
        kern,
        grid=(B // nb, T // tt),
        in_specs=[pl.BlockSpec((nb, tt, D), lambda b, t: (b, t, 0)),
                  pl.BlockSpec((nb, S), lambda b, t: (b, 0)),
                  pl.BlockSpec((nb, S), lambda b, t: (b, 0)),
                  full(wb), full(ar), full(ai), full(wc), full(d_skip)],
        out_specs=[pl.BlockSpec((nb, tt, D), lambda b, t: (b, t, 0)),
                   pl.BlockSpec((nb, S), lambda b, t: (b, 0)),
                   pl.BlockSpec((nb, S), lambda b, t: (b, 0))],
        out_shape=[jax.ShapeDtypeStruct((B, T, D), F32),
                   jax.ShapeDtypeStruct((B, S), F32),
                   jax.ShapeDtypeStruct((B, S), F32)],
        scratch_shapes=[pltpu.VMEM((D // LANES, nb * tt, LANES), F32),
                        pltpu.VMEM((S // LANES, nb * tt, LANES), F32), pltpu.VMEM((S // LANES, nb * tt, LANES), F32),
                        pltpu.VMEM((nb, S), F32), pltpu.VMEM((nb, S), F32),
                        pltpu.VMEM(wb.shape, BF16), pltpu.VMEM(wb.shape, BF16), pltpu.VMEM(wc.shape, BF16)],
        compiler_params=_params(("arbitrary", "arbitrary")),
        name=name,
    )(u3, h0r, h0i, wb, ar, ai, wc, d_skip)


def s5_tables(log_dt, a_re, a_im, b_re, b_im, c_re, c_im, n_chunk):
    ng, p = a_re.shape
    ch = b_re.shape[-1]
    dt = jnp.exp(log_dt)[:, None]
    mag, ang = jnp.exp(dt * a_re), dt * a_im
    abr, abi = mag * jnp.cos(ang), mag * jnp.sin(ang)
    den = a_re * a_re + a_im * a_im
    fr = ((abr - 1.0) * a_re + abi * a_im) / den
    fi = (abi * a_re - (abr - 1.0) * a_im) / den
    bbr = fr[..., None] * b_re - fi[..., None] * b_im
    bbi = fr[..., None] * b_im + fi[..., None] * b_re
    gpc = ng // n_chunk
    eye = jnp.eye(gpc, dtype=F32)

    def in_proj(bb):
        b4 = bb.reshape(n_chunk, gpc, p, ch)
        return jnp.einsum("jgpc,gh->jgchp", b4, eye).reshape(n_chunk, gpc * ch, gpc * p)

    def out_proj(cc):
        c4 = cc.reshape(n_chunk, gpc, ch, p)
        return jnp.einsum("jgcp,gh->jgphc", c4, eye).reshape(n_chunk, gpc * p, gpc * ch)

    wb = jnp.concatenate([in_proj(bbr), in_proj(bbi)], axis=2)
    wc = jnp.concatenate([out_proj(c_re), -out_proj(c_im)], axis=1)
    return wb, abr.reshape(1, ng * p), abi.reshape(1, ng * p), wc


def _router_kernel(x_ref, g_ref, sh_ref, sc_ref, w_ref, b_ref, h_ref, gate_ref, idx_ref, rank_ref, cnt_ref,
                   whi, wlo, count, *, bb, rows):
    @pl.when(pl.program_id(0) == 0)
    def _():
        w = w_ref[...]
        hi = w.astype(BF16)
        whi[...] = hi
        wlo[...] = (w - hi.astype(F32)).astype(BF16)
        count[...] = jnp.zeros(count.shape, F32)

    x = x_ref[...]
    ms = jnp.mean(x * x, axis=-1, keepdims=True)
    x = x * lax.rsqrt(ms + EPS) * g_ref[...]
    x = x * (1.0 + sc_ref[...]) + sh_ref[...]
    h_ref[...] = x
    tm = bb * rows
    x2 = x.reshape(tm, x.shape[-1])
    xh, xl = _split2(x2)
    logits = _dot(xh, whi[...]) + _dot(xl, whi[...]) + _dot(xh, wlo[...]) + b_ref[...]
    ne = logits.shape[1]
    lane = lax.broadcasted_iota(jnp.int32, logits.shape, 1).astype(F32)
    kcol = lax.broadcasted_iota(jnp.int32, (tm, TOP_K), 1)
    vals = jnp.zeros((tm, TOP_K), F32)
    idxs = jnp.zeros((tm, TOP_K), F32)
    cur = logits
    hot = []
    for k in range(TOP_K):
        mx = jnp.max(cur, axis=1, keepdims=True)
        am = jnp.min(jnp.where(cur == mx, lane, float(ne)), axis=1, keepdims=True)
        vals = jnp.where(kcol == k, mx, vals)
        idxs = jnp.where(kcol == k, am, idxs)
        sel = lane == am
        hot.append(jnp.where(sel, 1.0, 0.0))
        cur = jnp.where(sel, -jnp.inf, cur)
    e = jnp.exp(vals - vals[:, 0:1])
    gate_ref[...] = e / jnp.sum(e, axis=1, keepdims=True)
    idx_ref[...] = idxs.astype(jnp.int32)
    cnt = hot[0] + hot[1] + hot[2] + hot[3]
    r = lax.broadcasted_iota(jnp.int32, (tm, tm), 0)
    c = lax.broadcasted_iota(jnp.int32, (tm, tm), 1)
    before = jnp.where(c < r, 1.0, 0.0).astype(BF16)
    base = _dot(before, cnt.astype(BF16)) + count[...]
    ranks = jnp.zeros((tm, TOP_K), F32)
    for k in range(TOP_K):
        ranks = jnp.where(kcol == k, jnp.sum(hot[k] * base, axis=1, keepdims=True), ranks)
    rank_ref[...] = ranks.astype(jnp.int32)
    count[...] += jnp.sum(cnt, axis=0, keepdims=True)
    cnt_ref[...] = count[...]


def moe_router(x3, gain, layer, modexp, w_router, b_router):
    G, R, K = x3.shape
    E = w_router.shape[-1]
    bb = _pick(G, (64, 32, 16, 8, 4, 2, 1))
    tm = bb * R
    T = G * R
    kern = functools.partial(_router_kernel, bb=bb, rows=R)
    return pl.pallas_call(
        kern,
        grid=(G // bb,),
        in_specs=[pl.BlockSpec((bb, R, K), lambda g: (g, 0, 0)),
                  pl.BlockSpec((None, 1, K), lambda g: (layer, 0, 0)),
                  pl.BlockSpec((None, bb, 1, K), lambda g: (3, g, 0, 0)),
                  pl.BlockSpec((None, bb, 1, K), lambda g: (4, g, 0, 0)),
                  pl.BlockSpec((None, K, E), lambda g: (layer, 0, 0)),
                  pl.BlockSpec((None, 1, E), lambda g: (layer, 0, 0))],
        out_specs=[pl.BlockSpec((bb, R, K), lambda g: (g, 0, 0)),
                   pl.BlockSpec((tm, TOP_K), lambda g: (g, 0)),
                   pl.BlockSpec((tm, TOP_K), lambda g: (g, 0)),
                   pl.BlockSpec((tm, TOP_K), lambda g: (g, 0)),
                   pl.BlockSpec((1, E), lambda g: (0, 0))],
        out_shape=[jax.ShapeDtypeStruct((G, R, K), F32),
                   jax.ShapeDtypeStruct((T, TOP_K), F32),
                   jax.ShapeDtypeStruct((T, TOP_K), jnp.int32),
                   jax.ShapeDtypeStruct((T, TOP_K), jnp.int32),
                   jax.ShapeDtypeStruct((1, E), F32)],
        scratch_shapes=[pltpu.VMEM((K, E), BF16), pltpu.VMEM((K, E), BF16), pltpu.VMEM((1, E), F32)],
        compiler_params=_params(("arbitrary",)),
        name="moe_router",
    )(x3, gain.reshape(gain.shape[0], 1, K), modexp, modexp, w_router,
      b_router.reshape(b_router.shape[0], 1, E))


def _dispatch_kernel(dest_ref, h_ref, xs_in, xs_hbm, sem, *, tm):
    del xs_in
    i = pl.program_id(0)

    def issue(r, c):
        t = i * tm + r
        for k in range(TOP_K):
            pltpu.make_async_copy(h_ref.at[pl.ds(r, 1), :], xs_hbm.at[pl.ds(dest_ref[t * TOP_K + k], 1), :],
                                  sem.at[0]).start()
        return c

    lax.fori_loop(0, tm, issue, 0)
    for k in range(TOP_K):
        pltpu.make_async_copy(h_ref, xs_hbm.at[pl.ds(0, tm), :], sem.at[0]).wait()


def moe_dispatch(h2, dest, n_slots):
    T, D = h2.shape
    tm = _pick(T, (256, 128, 64, 32, 16, 8))
    xs0 = jnp.zeros((n_slots, D), F32)
    grid_spec = pltpu.PrefetchScalarGridSpec(
        num_scalar_prefetch=1,
        grid=(T // tm,),
        in_specs=[pl.BlockSpec((tm, D), lambda i, d: (i, 0)), pl.BlockSpec(memory_space=pl.ANY)],
        out_specs=pl.BlockSpec(memory_space=pl.ANY),
        scratch_shapes=[pltpu.SemaphoreType.DMA((1,))],
    )
    return pl.pallas_call(
        functools.partial(_dispatch_kernel, tm=tm),
        grid_spec=grid_spec,
        out_shape=jax.ShapeDtypeStruct((n_slots, D), F32),
        input_output_aliases={2: 0},
        compiler_params=_params(("arbitrary",)),
        name="moe_dispatch",
    )(dest, h2, xs0)


def _expert_kernel(bexp_ref, nblk_ref, x_ref, wgu_ref, bgu_ref, wd_ref, bd_ref, o_ref, wgu_bf, wd_bf, *, d_ff):
    i = pl.program_id(0)
    n_used = nblk_ref[0]

    @pl.when(i < n_used)
    def _():
        changed = jnp.logical_or(i == 0, bexp_ref[i] != bexp_ref[jnp.maximum(i - 1, 0)])

        @pl.when(changed)
        def _():
            wgu_bf[...] = wgu_ref[...].astype(BF16)
            wd_bf[...] = wd_ref[...].astype(BF16)

        gu = _dot(x_ref[...].astype(BF16), wgu_bf[...]) + bgu_ref[...]
        gate = jnp.minimum(gu[:, :d_ff], SWIGLU_LIMIT)
        up = jnp.clip(gu[:, d_ff:], -SWIGLU_LIMIT, SWIGLU_LIMIT)
        act = (up + 1.0) * gate * _sigmoid(SWIGLU_ALPHA * gate)
        o_ref[...] = _dot(act.astype(BF16), wd_bf[...]) + bd_ref[...]

    @pl.when(i >= n_used)
    def _():
        o_ref[...] = jnp.zeros(o_ref.shape, F32)


def moe_experts(xs, block_expert, n_used, w_gu, b_gu, w_down, b_down, *, layer, blk):
    n_slots, D = xs.shape
    n_blocks = n_slots // blk
    E, _, F2 = w_gu.shape[1:]
    d_ff = F2 // 2
    grid_spec = pltpu.PrefetchScalarGridSpec(
        num_scalar_prefetch=2,
        grid=(n_blocks,),
        in_specs=[pl.BlockSpec((blk, D), lambda i, be, nu: (jnp.minimum(i, nu[0] - 1), 0)),
                  pl.BlockSpec((None, None, D, F2), lambda i, be, nu: (layer, be[i], 0, 0)),
                  pl.BlockSpec((None, None, 1, F2), lambda i, be, nu: (layer, be[i], 0, 0)),
                  pl.BlockSpec((None, None, d_ff, D), lambda i, be, nu: (layer, be[i], 0, 0)),
                  pl.BlockSpec((None, None, 1, D), lambda i, be, nu: (layer, be[i], 0, 0))],
        out_specs=pl.BlockSpec((blk, D), lambda i, be, nu: (i, 0)),
        scratch_shapes=[pltpu.VMEM((D, F2), BF16), pltpu.VMEM((d_ff, D), BF16)],
    )
    return pl.pallas_call(
        functools.partial(_expert_kernel, d_ff=d_ff),
        grid_spec=grid_spec,
        out_shape=jax.ShapeDtypeStruct((n_slots, D), F32),
        compiler_params=_params(("arbitrary",)),
        name="moe_experts",
    )(block_expert, n_used, xs, w_gu, b_gu.reshape(b_gu.shape[0], E, 1, F2), w_down,
      b_down.reshape(b_down.shape[0], E, 1, D))


def _combine_kernel(dest_ref, yb_hbm, gates_ref, res_ref, gate_ref, o_ref, buf, sem, *, tm, bb, rows):
    i = pl.program_id(0)
    n = pl.num_programs(0)
    slot = i % 2

    def gather(tile, s):
        def issue(r, c):
            for k in range(TOP_K):
                src = dest_ref[(tile * tm + r) * TOP_K + k]
                pltpu.make_async_copy(yb_hbm.at[pl.ds(src, 1), :], buf.at[s, k, pl.ds(r, 1), :], sem.at[s]).start()
            return c
        lax.fori_loop(0, tm, issue, 0)

    @pl.when(i == 0)
    def _():
        gather(0, 0)

    @pl.when(i + 1 < n)
    def _():
        gather(i + 1, 1 - slot)

    for k in range(TOP_K):
        pltpu.make_async_copy(yb_hbm.at[pl.ds(0, tm), :], buf.at[slot, k], sem.at[slot]).wait()
    g = gates_ref[...]
    y = g[:, 0:1] * buf[slot, 0]
    for k in range(1, TOP_K):
        y = y + g[:, k:k + 1] * buf[slot, k]
    o_ref[...] = res_ref[...] + gate_ref[...] * y.reshape(bb, rows, y.shape[-1])


def moe_combine(yb, dest, gates, xres3, modexp):
    G, R, D = xres3.shape
    bb = _pick(G, (16, 8, 4, 2, 1))
    tm = bb * R
    grid_spec = pltpu.PrefetchScalarGridSpec(
        num_scalar_prefetch=1,
        grid=(G // bb,),
        in_specs=[pl.BlockSpec(memory_space=pl.ANY),
                  pl.BlockSpec((tm, TOP_K), lambda g, d: (g, 0)),
                  pl.BlockSpec((bb, R, D), lambda g, d: (g, 0, 0)),
                  pl.BlockSpec((None, bb, 1, D), lambda g, d: (5, g, 0, 0))],
        out_specs=pl.BlockSpec((bb, R, D), lambda g, d: (g, 0, 0)),
        scratch_shapes=[pltpu.VMEM((2, TOP_K, tm, D), F32), pltpu.SemaphoreType.DMA((2,))],
    )
    return pl.pallas_call(
        functools.partial(_combine_kernel, tm=tm, bb=bb, rows=R),
        grid_spec=grid_spec,
        out_shape=jax.ShapeDtypeStruct((G, R, D), F32),
        compiler_params=_params(("arbitrary",)),
        name="moe_combine",
    )(dest, yb, gates, xres3, modexp)


def moe_layer(x3, modexp, layer, norm_ffn, w_router, b_router, w_gu, b_gu, w_down, b_down, blk):
    G, R, D = x3.shape
    T = G * R
    E = w_router.shape[-1]
    h3, gates, idx, rank, counts = moe_router(x3, norm_ffn, layer, modexp, w_router, b_router)
    counts = counts.reshape(E).astype(jnp.int32)
    padded = (counts + blk - 1) // blk * blk
    pad_end = jnp.cumsum(padded)
    pad_start = pad_end - padded
    experts = jnp.arange(E, dtype=jnp.int32)
    dest = rank + jnp.sum(jnp.where(idx[..., None] == experts, pad_start, 0), axis=-1)
    dest = dest.reshape(-1).astype(jnp.int32)
    n_blocks = -(-(T * TOP_K) // blk) + E
    block_start = jnp.arange(n_blocks, dtype=jnp.int32) * blk
    block_expert = jnp.minimum(jnp.sum(pad_end[None, :] <= block_start[:, None], axis=1), E - 1).astype(jnp.int32)
    n_used = (pad_end[-1:] // blk).astype(jnp.int32)
    xs = moe_dispatch(h3.reshape(T, D), dest, n_blocks * blk)
    yb = moe_experts(xs, block_expert, n_used, w_gu, b_gu, w_down, b_down, layer=layer, blk=blk)
    return moe_combine(yb, dest, gates, x3, modexp)


def kernel(x_prompt, x_sample, cache_a_k, cache_a_v, state_b_re, state_b_im, cache_c_k, cache_c_v, page_table, c_prompt, c_sample, w_ada, b_ada, norm_mix, norm_ffn, a_w_in, a_q_norm, a_k_norm, a_lambda, a_sub_norm, a_w_out, b_log_dt, b_a_re, b_a_im, b_b_re, b_b_im, b_c_re, b_c_im, b_d, b_w_glu, b_b_glu, c_w_in, c_w_out, moe_w_router, moe_b_router, moe_w_gu, moe_b_gu, moe_w_down, moe_b_down):
    batch, seq, d = x_prompt.shape
    nseq, dec_seq, _ = x_sample.shape
    assert dec_seq == GROUP and seq % GROUP == 0
    depth = w_ada.shape[0]
    n_pages, page = page_table.shape[1], cache_a_k.shape[2]
    past = n_pages * page
    tp, ts = batch * seq, nseq * dec_seq
    T = tp + ts
    G = T // GROUP
    gp = tp // GROUP
    h_a, d_ha = cache_a_k.shape[3], cache_a_k.shape[5]
    h_c, d_hc = cache_c_k.shape[3], cache_c_k.shape[4]
    ssm_g, ssm_p = b_a_re.shape[1], b_a_re.shape[2]
    n_state = ssm_g * ssm_p

    x3 = jnp.concatenate([x_prompt.reshape(gp, GROUP, d), x_sample], axis=0)

    nc = batch + nseq
    ncp = -(-nc // GROUP) * GROUP
    c_all = jnp.concatenate([c_prompt, c_sample, jnp.zeros((ncp - nc, d), F32)], axis=0).reshape(1, ncp, d)
    modexps = []
    for i in range(depth):
        m = fused_matmul(c_all, w_ada, b_ada, layer=i, silu_in=True, passes=3, name="ada")[0]
        m = m.reshape(ncp, 6, d).transpose(1, 0, 2)
        mexp = jnp.concatenate([jnp.repeat(m[:, :batch], seq // GROUP, axis=1), m[:, batch:nc]], axis=1)
        modexps.append(mexp.reshape(6, G, 1, d))

    pos = jnp.concatenate([jnp.tile(jnp.arange(seq, dtype=jnp.int32), batch),
                           jnp.tile(past + jnp.arange(dec_seq, dtype=jnp.int32), nseq)])
    freqs = ROPE_THETA ** (-jnp.arange(0, d_ha, 2, dtype=F32) / d_ha)
    ang = pos.astype(F32)[:, None] * freqs[None, :]
    cos, sin = jnp.cos(ang), jnp.sin(ang)
    reps = LANES // d_ha
    cos_t = jnp.tile(jnp.concatenate([cos, cos], axis=1), (1, reps))
    sin_t = jnp.tile(jnp.concatenate([-sin, sin], axis=1), (1, reps))

    zeros_state = jnp.zeros((batch, n_state), F32)
    moe_blk = 256 if T * TOP_K >= 32768 else 128
    new_a, new_b, new_c = [], [], []
    for i in range(depth):
        kind, j = i % N_MIXERS, i // N_MIXERS
        mexp = modexps[i]
        if kind == 0:
            lam_init = 0.8 - 0.6 * math.exp(-0.3 * i)
            nqk = h_a * 2 * d_ha
            qkv3 = fused_matmul(x3, a_w_in, layer=j, norm=(norm_mix, i, mexp, 0, 1), name="a_qkv")
            qkv2 = qkv3.reshape(T, 3 * nqk)
            gains = jnp.stack([jnp.tile(a_q_norm[j], reps) * (d_ha ** -0.5), jnp.tile(a_k_norm[j], reps)])
            qk2 = qknorm_rope(qkv2, gains.reshape(2, 1, LANES), cos_t, sin_t, 2 * nqk, d_ha)
            o2 = diff_attention_prompt(qk2, qkv2, a_sub_norm, a_lambda, batch=batch, seq=seq, n_heads=h_a,
                                       d_head=d_ha, lam_init=lam_init, layer=j)
            o2 = diff_attention_sample(o2, qk2, qkv2, cache_a_k, cache_a_v, page_table, a_sub_norm, a_lambda,
                                       n_prompt_rows=tp, n_heads=h_a, d_head=d_ha, nq=dec_seq,
                                       lam_init=lam_init, layer=j)
            x3 = fused_matmul(o2.reshape(G, GROUP, nqk), a_w_out, layer=j, resid=(x3, mexp, 2), name="a_out")
            k2, v2 = qk2[:, nqk:], qkv2[:, 2 * nqk:]
            new_a.append((k2[:tp].reshape(batch, seq, h_a, 2, d_ha), v2[:tp].reshape(batch, seq, h_a, 2 * d_ha),
                          k2[tp:].reshape(nseq, dec_seq, h_a, 2, d_ha), v2[tp:].reshape(nseq, dec_seq, h_a, 2 * d_ha)))
        elif kind == 1:
            u3 = mod_norm(x3, norm_mix, i, mexp, 0, 1)
            wb, ar, ai, wc = s5_tables(b_log_dt[j], b_a_re[j], b_a_im[j], b_b_re[j], b_b_im[j],
                                       b_c_re[j], b_c_im[j], ssm_g // GROUP)
            d_skip = b_d[j].reshape(1, d)
            y_p, hpr, hpi = s5_scan(u3[:gp].reshape(batch, seq, d), zeros_state, zeros_state, wb, ar, ai, wc,
                                    d_skip, nb=batch, tt=_pick(seq, (64, 32, 16, 8)), passes=1, name="s5_prompt")
            y_s, hsr, hsi = s5_scan(u3[gp:], state_b_re[j].reshape(nseq, n_state), state_b_im[j].reshape(nseq, n_state),
                                    wb, ar, ai, wc, d_skip, nb=_pick(nseq, (32, 16, 8)), tt=dec_seq, passes=3,
                                    name="s5_sample")
            y3 = jnp.concatenate([y_p.reshape(gp, GROUP, d), y_s], axis=0)
            x3 = fused_matmul(y3, b_w_glu, b_b_glu, layer=j, glu=True, resid=(x3, mexp, 2), name="b_glu")
            st = lambda a, n: a.reshape(n, ssm_g, ssm_p)
            new_b.append((st(hpr, batch), st(hpi, batch), st(hsr, nseq), st(hsi, nseq)))
        else:
            dm = h_c * d_hc
            qkv3 = fused_matmul(x3, c_w_in, layer=j, norm=(norm_mix, i, mexp, 0, 1), name="c_qkv")
            qkv2 = qkv3.reshape(T, 3 * dm)
            o2 = sb_attention_prompt(qkv2, batch=batch, seq=seq, n_heads=h_c, d_head=d_hc)
            o2 = sb_attention_sample(o2, qkv2, cache_c_k, cache_c_v, page_table, n_prompt_rows=tp,
                                     n_heads=h_c, d_head=d_hc, nq=dec_seq, layer=j)
            x3 = fused_matmul(o2.reshape(G, GROUP, dm), c_w_out, layer=j, resid=(x3, mexp, 2), name="c_out")
            k2, v2 = qkv2[:, dm:2 * dm], qkv2[:, 2 * dm:]
            new_c.append((k2[:tp].reshape(batch, seq, h_c, d_hc), v2[:tp].reshape(batch, seq, h_c, d_hc),
                          k2[tp:].reshape(nseq, dec_seq, h_c, d_hc), v2[tp:].reshape(nseq, dec_seq, h_c, d_hc)))
        x3 = moe_layer(x3, mexp, i, norm_ffn, moe_w_router, moe_b_router, moe_w_gu, moe_b_gu,
                       moe_w_down, moe_b_down, moe_blk)

    outs = [x3[:gp].reshape(batch, seq, d), x3[gp:]]
    for group in (new_a, new_b, new_c):
        for k in range(4):
            outs.append(jnp.stack([entry[k] for entry in group]))
    return tuple(outs)
```

```python
import functools
import math

import jax
import jax.numpy as jnp
from jax import lax
from jax.experimental import pallas as pl
from jax.experimental.pallas import tpu as pltpu

F32 = jnp.float32
BF16 = jnp.bfloat16

EPS = 1e-6
NEG_INF = -1e30
ROPE_THETA = 10000.0
TOP_K = 4
SWIGLU_LIMIT = 7.0
SWIGLU_ALPHA = 1.702
N_MIXERS = 3
GROUP = 8
LANES = 128
VMEM_LIMIT = 56 * 1024 * 1024


def _params(sem, vmem=VMEM_LIMIT):
    return pltpu.CompilerParams(dimension_semantics=sem, vmem_limit_bytes=vmem)


def _pick(n, cands):
    for c in cands:
        if n % c == 0:
            return c
    raise ValueError(f"no tile for {n} in {cands}")


def _split2(x):
    hi = x.astype(BF16)
    lo = (x - hi.astype(F32)).astype(BF16)
    return hi, lo


def _sigmoid(x):
    return 1.0 / (1.0 + jnp.exp(-x))


def _dot(a, b):
    return jnp.dot(a, b, preferred_element_type=F32)


def _dot_nt(a, b):
    return lax.dot_general(a, b, (((1,), (1,)), ((), ())), preferred_element_type=F32)


def _mm_kernel(*refs, norm, silu_in, glu, has_bias, resid, passes, bb, rows):
    it = iter(refs)
    x_ref = next(it)
    if norm:
        g_ref, sh_ref, sc_ref = next(it), next(it), next(it)
    w_refs = [next(it)] + ([next(it)] if glu else [])
    b_refs = ([next(it)] + ([next(it)] if glu else [])) if has_bias else []
    if resid:
        res_ref, gate_ref = next(it), next(it)
    o_ref = next(it)
    whi = [next(it) for _ in w_refs]
    wlo = [next(it) for _ in w_refs] if passes == 3 else []

    @pl.when(pl.program_id(1) == 0)
    def _():
        for i, w_ref in enumerate(w_refs):
            w = w_ref[...]
            hi = w.astype(BF16)
            whi[i][...] = hi
            if passes == 3:
                wlo[i][...] = (w - hi.astype(F32)).astype(BF16)

    x = x_ref[...]
    if silu_in:
        x = x * _sigmoid(x)
    if norm:
        ms = jnp.mean(x * x, axis=-1, keepdims=True)
        x = x * lax.rsqrt(ms + EPS) * g_ref[...]
        x = x * (1.0 + sc_ref[...]) + sh_ref[...]
    x2 = x.reshape(bb * rows, x.shape[-1])
    xh = x2.astype(BF16)
    xl = (x2 - xh.astype(F32)).astype(BF16) if passes == 3 else None

    def mm(i):
        acc = _dot(xh, whi[i][...])
        if passes == 3:
            acc = acc + _dot(xl, whi[i][...]) + _dot(xh, wlo[i][...])
        if has_bias:
            acc = acc + b_refs[i][...]
        return acc

    z = mm(0)
    if glu:
        z = z * _sigmoid(mm(1))
    z3 = z.reshape(bb, rows, z.shape[-1])
    if resid:
        z3 = res_ref[...] + gate_ref[...] * z3
    o_ref[...] = z3


def fused_matmul(x3, w, b=None, *, layer=0, norm=None, silu_in=False, glu=False,
                 resid=None, passes=1, tn=None, name="mm"):
    G, R, K = x3.shape
    N = w.shape[-1]
    n_out = N // 2 if glu else N
    bb = _pick(G, (64, 32, 16, 8, 4, 2, 1))
    tn = tn or min(n_out, 1024)
    assert n_out % tn == 0
    half = n_out // tn

    in_specs = [pl.BlockSpec((bb, R, K), lambda n, g: (g, 0, 0))]
    args = [x3]
    if norm is not None:
        gain, gl, modexp, i_sh, i_sc = norm
        gain3 = gain.reshape(gain.shape[0], 1, K)
        in_specs += [pl.BlockSpec((None, 1, K), lambda n, g: (gl, 0, 0)),
                     pl.BlockSpec((None, bb, 1, K), lambda n, g: (i_sh, g, 0, 0)),
                     pl.BlockSpec((None, bb, 1, K), lambda n, g: (i_sc, g, 0, 0))]
        args += [gain3, modexp, modexp]
    in_specs.append(pl.BlockSpec((None, K, tn), lambda n, g: (layer, 0, n)))
    args.append(w)
    if glu:
        in_specs.append(pl.BlockSpec((None, K, tn), lambda n, g: (layer, 0, n + half)))
        args.append(w)
    if b is not None:
        b3 = b.reshape(b.shape[0], 1, N)
        in_specs.append(pl.BlockSpec((None, 1, tn), lambda n, g: (layer, 0, n)))
        args.append(b3)
        if glu:
            in_specs.append(pl.BlockSpec((None, 1, tn), lambda n, g: (layer, 0, n + half)))
            args.append(b3)
    if resid is not None:
        xres, modexp_r, i_g = resid
        in_specs += [pl.BlockSpec((bb, R, tn), lambda n, g: (g, 0, n)),
                     pl.BlockSpec((None, bb, 1, tn), lambda n, g: (i_g, g, 0, n))]
        args += [xres, modexp_r]
    nw = 2 if glu else 1
    scratch = [pltpu.VMEM((K, tn), BF16) for _ in range(nw * (2 if passes == 3 else 1))]
    kern = functools.partial(_mm_kernel, norm=norm is not None, silu_in=silu_in, glu=glu,
                             has_bias=b is not None, resid=resid is not None, passes=passes,
                             bb=bb, rows=R)
    return pl.pallas_call(
        kern,
        grid=(n_out // tn, G // bb),
        in_specs=in_specs,
        out_specs=pl.BlockSpec((bb, R, tn), lambda n, g: (g, 0, n)),
        out_shape=jax.ShapeDtypeStruct((G, R, n_out), F32),
        scratch_shapes=scratch,
        compiler_params=_params(("arbitrary", "arbitrary")),
        name=name,
    )(*args)


def _mod_norm_kernel(x_ref, g_ref, sh_ref, sc_ref, o_ref):
    x = x_ref[...]
    ms = jnp.mean(x * x, axis=-1, keepdims=True)
    x = x * lax.rsqrt(ms + EPS) * g_ref[...]
    o_ref[...] = x * (1.0 + sc_ref[...]) + sh_ref[...]


def mod_norm(x3, gain, layer, modexp, i_sh, i_sc):
    G, R, K = x3.shape
    bb = _pick(G, (64, 32, 16, 8, 4, 2, 1))
    return pl.pallas_call(
        _mod_norm_kernel,
        grid=(G // bb,),
        in_specs=[pl.BlockSpec((bb, R, K), lambda g: (g, 0, 0)),
                  pl.BlockSpec((None, 1, K), lambda g: (layer, 0, 0)),
                  pl.BlockSpec((None, bb, 1, K), lambda g: (i_sh, g, 0, 0)),
                  pl.BlockSpec((None, bb, 1, K), lambda g: (i_sc, g, 0, 0))],
        out_specs=pl.BlockSpec((bb, R, K), lambda g: (g, 0, 0)),
        out_shape=jax.ShapeDtypeStruct((G, R, K), F32),
        compiler_params=_params(("parallel",)),
        name="mod_norm",
    )(x3, gain.reshape(gain.shape[0], 1, K), modexp, modexp)


def _qknorm_rope_kernel(x_ref, g_ref, cos_ref, sin_ref, o_ref, *, d_head):
    x = x_ref[...]
    r = lax.broadcasted_iota(jnp.int32, (LANES, LANES), 0) // d_head
    c = lax.broadcasted_iota(jnp.int32, (LANES, LANES), 1) // d_head
    seg = jnp.where(r == c, 1.0, 0.0).astype(BF16)
    hi, lo = _split2(x * x)
    ss = _dot(hi, seg) + _dot(lo, seg)
    y = x * lax.rsqrt(ss * (1.0 / d_head) + EPS) * g_ref[...]
    lane = lax.broadcasted_iota(jnp.int32, x.shape, 1)
    half = d_head // 2
    rot = jnp.where(lane % d_head < half, pltpu.roll(y, LANES - half, axis=1), pltpu.roll(y, half, axis=1))
    o_ref[...] = y * cos_ref[...] + rot * sin_ref[...]


def qknorm_rope(qkv2, gains, cos_t, sin_t, n_qk_cols, d_head):
    T = qkv2.shape[0]
    tm = _pick(T, (1024, 512, 256, 128, 64, 32, 16, 8))
    ncb = n_qk_cols // LANES
    per = ncb // 2
    return pl.pallas_call(
        functools.partial(_qknorm_rope_kernel, d_head=d_head),
        grid=(T // tm, ncb),
        in_specs=[pl.BlockSpec((tm, LANES), lambda i, j: (i, j)),
                  pl.BlockSpec((None, 1, LANES), lambda i, j: (j // per, 0, 0)),
                  pl.BlockSpec((tm, LANES), lambda i, j: (i, 0)),
                  pl.BlockSpec((tm, LANES), lambda i, j: (i, 0))],
        out_specs=pl.BlockSpec((tm, LANES), lambda i, j: (i, j)),
        out_shape=jax.ShapeDtypeStruct((T, n_qk_cols), F32),
        compiler_params=_params(("parallel", "parallel")),
        name="qknorm_rope",
    )(qkv2, gains, cos_t, sin_t)


def _lambda_value(alam_ref, lam_init):
    a = alam_ref[...]
    s1 = jnp.sum(a[0:1] * a[1:2], axis=1, keepdims=True)
    s2 = jnp.sum(a[2:3] * a[3:4], axis=1, keepdims=True)
    return jnp.exp(s1) - jnp.exp(s2) + lam_init


def _softmax_update(s, v_bf, m_ref, l_ref, acc_ref, reps_s, reps_acc):
    m_prev = m_ref[...]
    m_new = jnp.maximum(m_prev, jnp.max(s, axis=1, keepdims=True))
    alpha = jnp.exp(m_prev - m_new)
    m_b = m_new if reps_s == 1 else jnp.concatenate([m_new] * reps_s, axis=1)
    p = jnp.exp(s - m_b)
    l_ref[...] = alpha * l_ref[...] + jnp.sum(p, axis=1, keepdims=True)
    a_b = alpha if reps_acc == 1 else jnp.concatenate([alpha] * reps_acc, axis=1)
    acc_ref[...] = a_b * acc_ref[...] + _dot(p.astype(BF16), v_bf)
    m_ref[...] = m_new


def _diff_prompt_kernel(q_ref, k_ref, v_ref, g_ref, alam_ref, o_ref,
                        kbf, vbf, m0, l0, a0, m1, l1, a1, *, tq, d_head, lam_init):
    qi = pl.program_id(2)

    @pl.when(qi == 0)
    def _():
        kbf[...] = k_ref[...].astype(BF16)
        vbf[...] = v_ref[...].astype(BF16)

    q = q_ref[...]
    lane = lax.broadcasted_iota(jnp.int32, q.shape, 1)
    qc = (jnp.where(lane < d_head, q, 0.0).astype(BF16), jnp.where(lane >= d_head, q, 0.0).astype(BF16))
    stats = ((m0, l0, a0), (m1, l1, a1))
    for m, l, a in stats:
        m[...] = jnp.full(m.shape, NEG_INF, F32)
        l[...] = jnp.zeros(l.shape, F32)
        a[...] = jnp.zeros(a.shape, F32)
    reps = tq // LANES

    def chunk(kj, masked):
        off = pl.multiple_of(kj * tq, tq)
        kb = kbf[pl.ds(off, tq), :]
        vb = vbf[pl.ds(off, tq), :]
        for c in range(2):
            s = _dot_nt(qc[c], kb)
            if masked:
                row = lax.broadcasted_iota(jnp.int32, s.shape, 0)
                col = lax.broadcasted_iota(jnp.int32, s.shape, 1)
                s = jnp.where(col <= row, s, NEG_INF)
            _softmax_update(s, vb, *stats[c], reps, 1)

    def body(kj, carry):
        chunk(kj, False)
        return carry

    lax.fori_loop(0, qi, body, 0)
    chunk(qi, True)

    lam = _lambda_value(alam_ref, lam_init)
    o = a0[...] / l0[...] - lam * (a1[...] / l1[...])
    ms = jnp.mean(o * o, axis=1, keepdims=True)
    o_ref[...] = o * lax.rsqrt(ms + EPS) * g_ref[...] * (1.0 - lam_init)


def diff_attention_prompt(qk2, qkv2, sub_gain, alam, *, batch, seq, n_heads, d_head, lam_init, layer):
    T = qk2.shape[0]
    dv = 2 * d_head
    assert dv == LANES
    tq = _pick(seq, (512, 256, 128))
    nq = seq // tq
    kcol = n_heads
    vcol = 2 * n_heads
    kern = functools.partial(_diff_prompt_kernel, tq=tq, d_head=d_head, lam_init=lam_init)
    return pl.pallas_call(
        kern,
        grid=(batch, n_heads, nq),
        in_specs=[pl.BlockSpec((tq, LANES), lambda b, h, i: (b * nq + i, h)),
                  pl.BlockSpec((seq, LANES), lambda b, h, i: (b, kcol + h)),
                  pl.BlockSpec((seq, LANES), lambda b, h, i: (b, vcol + h)),
                  pl.BlockSpec((None, 1, LANES), lambda b, h, i: (layer, 0, 0)),
                  pl.BlockSpec((None, 4, d_head), lambda b, h, i: (layer, 0, 0))],
        out_specs=pl.BlockSpec((tq, LANES), lambda b, h, i: (b * nq + i, h)),
        out_shape=jax.ShapeDtypeStruct((T, n_heads * dv), F32),
        scratch_shapes=[pltpu.VMEM((seq, LANES), BF16), pltpu.VMEM((seq, LANES), BF16)]
        + [pltpu.VMEM((tq, LANES), F32) for _ in range(6)],
        compiler_params=_params(("arbitrary", "arbitrary", "arbitrary")),
        name="diff_attn_prompt",
    )(qk2, qk2, qkv2, sub_gain.reshape(sub_gain.shape[0], 1, dv), alam)


def _diff_sample_kernel(pt_ref, q_ref, kn_ref, vn_ref, *rest, n_steps, pps, n_heads, d_head, nq, lam_init):
    del pt_ref
    kp_refs, vp_refs = rest[:pps], rest[pps:2 * pps]
    g_ref, alam_ref, oin_ref, o_ref, qbd, m, l, acc = rest[2 * pps:]
    del oin_ref
    p = pl.program_id(1)
    dm = n_heads * 2 * d_head
    dv = 2 * d_head
    rph = 2 * nq
    nrow = n_heads * rph

    @pl.when(p == 0)
    def _():
        row = lax.broadcasted_iota(jnp.int32, (nrow, dm), 0)
        col = lax.broadcasted_iota(jnp.int32, (nrow, dm), 1)
        qt = jnp.concatenate([q_ref[...]] * (2 * n_heads), axis=0)
        qbd[...] = jnp.where(col // d_head == row // nq, qt, 0.0).astype(BF16)
        m[...] = jnp.full(m.shape, NEG_INF, F32)
        l[...] = jnp.zeros(l.shape, F32)
        acc[...] = jnp.zeros(acc.shape, F32)

    def update(s, v_of_head):
        reps = s.shape[1] // LANES
        m_prev = m[...]
        m_new = jnp.maximum(m_prev, jnp.max(s, axis=1, keepdims=True))
        alpha = jnp.exp(m_prev - m_new)
        pr = jnp.exp(s - (m_new if reps == 1 else jnp.concatenate([m_new] * reps, axis=1)))
        l[...] = alpha * l[...] + jnp.sum(pr, axis=1, keepdims=True)
        pb = pr.astype(BF16)
        pv = jnp.concatenate([_dot(pb[h * rph:(h + 1) * rph], v_of_head(h)) for h in range(n_heads)], axis=0)
        acc[...] = alpha * acc[...] + pv
        m[...] = m_new

    qb = qbd[...]
    s = jnp.concatenate([_dot(qb, kp_refs[i][...].astype(BF16)) for i in range(pps)], axis=1)
    update(s, lambda h: jnp.concatenate(
        [vp_refs[i][pl.ds(h, LANES, stride=n_heads), :].astype(BF16) for i in range(pps)], axis=0))

    @pl.when(p == n_steps - 1)
    def _():
        kn = jnp.concatenate([kn_ref[...], jnp.zeros((LANES - nq, dm), F32)], axis=0).astype(BF16)
        r = lax.broadcasted_iota(jnp.int32, (nrow, LANES), 0)
        j = lax.broadcasted_iota(jnp.int32, (nrow, LANES), 1)
        s = jnp.where(j <= r % nq, _dot_nt(qbd[...], kn), NEG_INF)
        vpad = jnp.zeros((LANES - nq, dv), F32)
        update(s, lambda h: jnp.concatenate([vn_ref[:, h * dv:(h + 1) * dv], vpad], axis=0).astype(BF16))
        lam = _lambda_value(alam_ref, lam_init)
        on = acc[...] / l[...]
        for h in range(n_heads):
            oh = on[h * rph:h * rph + nq] - lam * on[h * rph + nq:(h + 1) * rph]
            ms = jnp.mean(oh * oh, axis=1, keepdims=True)
            o_ref[:, h * dv:(h + 1) * dv] = oh * lax.rsqrt(ms + EPS) * g_ref[...] * (1.0 - lam_init)


def diff_attention_sample(o_prompt, qk2, qkv2, cache_k, cache_v, page_table, sub_gain, alam, *,
                          n_prompt_rows, n_heads, d_head, nq, lam_init, layer):
    T, dm = o_prompt.shape
    nseq, n_pages = page_table.shape
    n_layers, n_pool, page = cache_k.shape[:3]
    dv = 2 * d_head
    assert page == LANES and 2 * n_heads * nq == LANES and nq == GROUP and dv == LANES
    ck = jnp.transpose(cache_k, (0, 1, 3, 4, 5, 2)).reshape(n_layers, n_pool, dm, page)
    cv = cache_v.reshape(n_layers, n_pool, page * n_heads, dv)
    r0 = n_prompt_rows // nq
    pps = _pick(n_pages, (8, 4, 2, 1))
    n_steps = n_pages // pps

    def page_map(i):
        return lambda b, p, pt: (layer, pt[b * n_pages + p * pps + i], 0, 0)

    in_specs = [pl.BlockSpec((nq, dm), lambda b, p, pt: (r0 + b, 0)),
                pl.BlockSpec((nq, dm), lambda b, p, pt: (r0 + b, 1)),
                pl.BlockSpec((nq, dm), lambda b, p, pt: (r0 + b, 2))]
    in_specs += [pl.BlockSpec((None, None, dm, page), page_map(i)) for i in range(pps)]
    in_specs += [pl.BlockSpec((None, None, page * n_heads, dv), page_map(i)) for i in range(pps)]
    in_specs += [pl.BlockSpec((None, 1, dv), lambda b, p, pt: (layer, 0, 0)),
                 pl.BlockSpec((None, 4, d_head), lambda b, p, pt: (layer, 0, 0)),
                 pl.BlockSpec(memory_space=pl.ANY)]
    grid_spec = pltpu.PrefetchScalarGridSpec(
        num_scalar_prefetch=1,
        grid=(nseq, n_steps),
        in_specs=in_specs,
        out_specs=pl.BlockSpec((nq, dm), lambda b, p, pt: (r0 + b, 0)),
        scratch_shapes=[pltpu.VMEM((LANES, dm), BF16), pltpu.VMEM((LANES, LANES), F32),
                        pltpu.VMEM((LANES, LANES), F32), pltpu.VMEM((LANES, dv), F32)],
    )
    kern = functools.partial(_diff_sample_kernel, n_steps=n_steps, pps=pps, n_heads=n_heads, d_head=d_head,
                             nq=nq, lam_init=lam_init)
    return pl.pallas_call(
        kern,
        grid_spec=grid_spec,
        out_shape=jax.ShapeDtypeStruct((T, dm), F32),
        input_output_aliases={6 + 2 * pps: 0},
        compiler_params=_params(("arbitrary", "arbitrary")),
        name="diff_attn_sample",
    )(page_table.reshape(-1), qk2, qk2, qkv2, *([ck] * pps), *([cv] * pps),
      sub_gain.reshape(sub_gain.shape[0], 1, dv), alam, o_prompt)


def _later_matrix(n):
    j = lax.broadcasted_iota(jnp.int32, (n, n), 0)
    s = lax.broadcasted_iota(jnp.int32, (n, n), 1)
    return jnp.where(j > s, 1.0, 0.0).astype(BF16)


def _stick_block(zn, log_stay, carry, later, mask):
    reps = zn.shape[1] // LANES
    hi, lo = _split2(log_stay)
    c_b = carry if reps == 1 else jnp.concatenate([carry] * reps, axis=1)
    log_later = _dot(hi, later) + _dot(lo, later) + c_b
    w = jnp.exp(log_stay - zn + log_later)
    if mask is not None:
        w = jnp.where(mask, w, 0.0)
    return w, carry + jnp.sum(log_stay, axis=1, keepdims=True)


def _log_stay(zn, mask):
    ls = jnp.minimum(zn, 0.0) - jnp.log(1.0 + jnp.exp(-jnp.abs(zn)))
    return ls if mask is None else jnp.where(mask, ls, 0.0)


def _sb_prompt_kernel(q_ref, k_ref, v_ref, o_ref, kbf, vbf, carry0, acc0, carry1, acc1, *, tq, cb, d_head, scale):
    qi = pl.program_id(2)

    @pl.when(qi == 0)
    def _():
        kbf[...] = k_ref[...].astype(BF16)
        vbf[...] = v_ref[...].astype(BF16)

    q = q_ref[...] * (-scale)
    lane = lax.broadcasted_iota(jnp.int32, q.shape, 1)
    first = lane < d_head
    qh = (jnp.where(first, q, 0.0).astype(BF16), jnp.where(first, 0.0, q).astype(BF16))
    later = _later_matrix(cb)
    state = ((carry0, acc0), (carry1, acc1))
    for c_ref, a_ref in state:
        c_ref[...] = jnp.zeros(c_ref.shape, F32)
        a_ref[...] = jnp.zeros(a_ref.shape, F32)

    def chunk(kj, masked):
        off = pl.multiple_of(kj * tq, tq)
        kb = kbf[pl.ds(off, tq), :]
        vb = vbf[pl.ds(off, tq), :]
        mask = None
        if masked:
            row = lax.broadcasted_iota(jnp.int32, (tq, tq), 0)
            col = lax.broadcasted_iota(jnp.int32, (tq, tq), 1)
            mask = col < row
        for h in range(2):
            c_ref, a_ref = state[h]
            zn = _dot_nt(qh[h], kb)
            ls = _log_stay(zn, mask)
            c = c_ref[...]
            acc = a_ref[...]
            for b in reversed(range(tq // cb)):
                sl = slice(b * cb, (b + 1) * cb)
                w, c = _stick_block(zn[:, sl], ls[:, sl], c, later, None if mask is None else mask[:, sl])
                acc = acc + _dot(w.astype(BF16), vb[sl, :])
            c_ref[...] = c
            a_ref[...] = acc

    chunk(qi, True)

    def body(t, c):
        chunk(qi - 1 - t, False)
        return c

    lax.fori_loop(0, qi, body, 0)
    o_ref[...] = jnp.where(first, acc0[...], acc1[...])


def sb_attention_prompt(qkv2, *, batch, seq, n_heads, d_head):
    T = qkv2.shape[0]
    dm = n_heads * d_head
    tq = _pick(seq, (512, 256, 128))
    cb = min(256, tq)
    nq = seq // tq
    npair = dm // LANES
    kern = functools.partial(_sb_prompt_kernel, tq=tq, cb=cb, d_head=d_head, scale=d_head ** -0.5)
    return pl.pallas_call(
        kern,
        grid=(batch, npair, nq),
        in_specs=[pl.BlockSpec((tq, LANES), lambda b, h, i: (b * nq + i, h)),
                  pl.BlockSpec((seq, LANES), lambda b, h, i: (b, npair + h)),
                  pl.BlockSpec((seq, LANES), lambda b, h, i: (b, 2 * npair + h))],
        out_specs=pl.BlockSpec((tq, LANES), lambda b, h, i: (b * nq + i, h)),
        out_shape=jax.ShapeDtypeStruct((T, dm), F32),
        scratch_shapes=[pltpu.VMEM((seq, LANES), BF16), pltpu.VMEM((seq, LANES), BF16)]
        + [pltpu.VMEM((tq, LANES), F32) for _ in range(4)],
        compiler_params=_params(("arbitrary", "arbitrary", "arbitrary")),
        name="sb_attn_prompt",
    )(qkv2, qkv2, qkv2)


def _sb_sample_kernel(pt_ref, q_ref, kn_ref, vn_ref, *rest, n_steps, pps, n_heads, d_head, nq, scale):
    del pt_ref
    kp_refs, vp_refs = rest[:pps], rest[pps:2 * pps]
    oin_ref, o_ref, qbd, carry, acc = rest[2 * pps:]
    del oin_ref
    p = pl.program_id(1)
    dm = n_heads * d_head
    nrow = n_heads * nq
    later = _later_matrix(LANES)

    @pl.when(p == 0)
    def _():
        row = lax.broadcasted_iota(jnp.int32, (nrow, dm), 0)
        col = lax.broadcasted_iota(jnp.int32, (nrow, dm), 1)
        qt = jnp.concatenate([q_ref[...] * (-scale)] * n_heads, axis=0)
        qbd[...] = jnp.where(col // d_head == row // nq, qt, 0.0).astype(BF16)
        pad = jnp.zeros((LANES - nq, dm), F32)
        kn = jnp.concatenate([kn_ref[...], pad], axis=0).astype(BF16)
        vn = jnp.concatenate([vn_ref[...], pad], axis=0).astype(BF16)
        r = lax.broadcasted_iota(jnp.int32, (nrow, LANES), 0)
        j = lax.broadcasted_iota(jnp.int32, (nrow, LANES), 1)
        mask = j < r % nq
        zn = _dot_nt(qbd[...], kn)
        w, c = _stick_block(zn, _log_stay(zn, mask), jnp.zeros((nrow, LANES), F32), later, mask)
        acc[...] = _dot(w.astype(BF16), vn)
        carry[...] = c

    qb = qbd[...]
    c = carry[...]
    ws = []
    for i in range(pps):
        zn = _dot(qb, kp_refs[i][...].astype(BF16))
        w, c = _stick_block(zn, _log_stay(zn, None), c, later, None)
        ws.append(w.astype(BF16))
    carry[...] = c
    wcat = ws[0] if pps == 1 else jnp.concatenate(ws, axis=1)
    vcat = jnp.concatenate([vp_refs[i][...].astype(BF16) for i in range(pps)], axis=1)
    acc[...] += _dot_nt(wcat, vcat)

    @pl.when(p == n_steps - 1)
    def _():
        row = lax.broadcasted_iota(jnp.int32, (nrow, dm), 0)
        col = lax.broadcasted_iota(jnp.int32, (nrow, dm), 1)
        comb = jnp.where(col // d_head == row // nq, acc[...], 0.0)
        o = comb[0:nq]
        for h in range(1, n_heads):
            o = o + comb[h * nq:(h + 1) * nq]
        o_ref[...] = o


def sb_attention_sample(o_prompt, qkv2, cache_k, cache_v, page_table, *, n_prompt_rows, n_heads, d_head, nq, layer):
    T, dm = o_prompt.shape
    nseq, n_pages = page_table.shape
    n_layers, n_pool, page = cache_k.shape[:3]
    assert page == LANES and n_heads * nq == LANES and nq == GROUP
    ck = jnp.transpose(cache_k, (0, 1, 3, 4, 2)).reshape(n_layers, n_pool, dm, page)
    cv = jnp.transpose(cache_v, (0, 1, 3, 4, 2)).reshape(n_layers, n_pool, dm, page)
    r0 = n_prompt_rows // nq
    pps = _pick(n_pages, (8, 4, 2, 1))
    n_steps = n_pages // pps

    def page_map(i):
        return lambda b, p, pt: (layer, pt[b * n_pages + n_pages - 1 - (p * pps + i)], 0, 0)

    in_specs = [pl.BlockSpec((nq, dm), lambda b, p, pt: (r0 + b, 0)),
                pl.BlockSpec((nq, dm), lambda b, p, pt: (r0 + b, 1)),
                pl.BlockSpec((nq, dm), lambda b, p, pt: (r0 + b, 2))]
    in_specs += [pl.BlockSpec((None, None, dm, page), page_map(i)) for i in range(pps)] * 2
    in_specs += [pl.BlockSpec(memory_space=pl.ANY)]
    grid_spec = pltpu.PrefetchScalarGridSpec(
        num_scalar_prefetch=1,
        grid=(nseq, n_steps),
        in_specs=in_specs,
        out_specs=pl.BlockSpec((nq, dm), lambda b, p, pt: (r0 + b, 0)),
        scratch_shapes=[pltpu.VMEM((LANES, dm), BF16), pltpu.VMEM((LANES, LANES), F32),
                        pltpu.VMEM((LANES, dm), F32)],
    )
    kern = functools.partial(_sb_sample_kernel, n_steps=n_steps, pps=pps, n_heads=n_heads, d_head=d_head,
                             nq=nq, scale=d_head ** -0.5)
    return pl.pallas_call(
        kern,
        grid_spec=grid_spec,
        out_shape=jax.ShapeDtypeStruct((T, dm), F32),
        input_output_aliases={4 + 2 * pps: 0},
        compiler_params=_params(("arbitrary", "arbitrary")),
        name="sb_attn_sample",
    )(page_table.reshape(-1), qkv2, qkv2, qkv2, *([ck] * pps), *([cv] * pps), o_prompt)


def _gelu_tanh(x):
    return 0.5 * x * (1.0 + jnp.tanh(math.sqrt(2.0 / math.pi) * (x + 0.044715 * x * x * x)))


def _s5_kernel(u_ref, h0r_ref, h0i_ref, wb_ref, ar_ref, ai_ref, wc_ref, d_ref,
               y_ref, hr_ref, hi_ref,
               utm, bur, bui, str_, sti, wbhi, wblo, wcbf, *, nb, tt, n_chunk, slabs_per_step, passes):
    t_idx = pl.program_id(1)
    rows = nb * tt
    dm = u_ref.shape[-1]
    ck = dm // n_chunk
    n_slab = bur.shape[0]
    spc = n_slab // n_chunk
    sk = spc * LANES

    @pl.when((pl.program_id(0) == 0) & (t_idx == 0))
    def _():
        wb = wb_ref[...]
        hi = wb.astype(BF16)
        wbhi[...] = hi
        wblo[...] = (wb - hi.astype(F32)).astype(BF16)
        wcbf[...] = wc_ref[...].astype(BF16)

    @pl.when(t_idx == 0)
    def _():
        str_[...] = h0r_ref[...]
        sti[...] = h0i_ref[...]

    assert ck == LANES
    for s in range(nb):
        for j in range(n_chunk):
            utm[j, pl.ds(s, tt, stride=nb), :] = u_ref[s, :, j * ck:(j + 1) * ck]

    for j in range(n_chunk):
        uh, ul = _split2(utm[j])
        bu = _dot(uh, wbhi[j])
        if passes == 3:
            bu = bu + _dot(ul, wbhi[j]) + _dot(uh, wblo[j])
        for q in range(spc):
            bur[j * spc + q] = bu[:, q * LANES:(q + 1) * LANES]
            bui[j * spc + q] = bu[:, sk + q * LANES:sk + (q + 1) * LANES]

    spv = min(nb, GROUP)
    spr = GROUP // spv
    assert spr in (1, 2) and nb % spv == 0
    low = lax.broadcasted_iota(jnp.int32, (GROUP, LANES), 0) < spv
    rep = (lambda x: x) if spr == 1 else (lambda x: jnp.concatenate([x, x], axis=0))
    for g in range(nb // spv):
        for s0 in range(0, n_slab, slabs_per_step):
            sl = [slice((s0 + q) * LANES, (s0 + q + 1) * LANES) for q in range(slabs_per_step)]
            ar = [jnp.broadcast_to(ar_ref[:, c], (GROUP, LANES)) for c in sl]
            ai = [jnp.broadcast_to(ai_ref[:, c], (GROUP, LANES)) for c in sl]
            h0 = (tuple(rep(str_[g * spv:(g + 1) * spv, c]) for c in sl)
                  + tuple(rep(sti[g * spv:(g + 1) * spv, c]) for c in sl))

            def step(tv, h, g=g, s0=s0, ar=ar, ai=ai):
                rows_t = pl.ds(pl.multiple_of(tv * (spr * nb) + g * GROUP, GROUP), GROUP)
                out_r, out_i = [], []
                for q in range(slabs_per_step):
                    h_r, h_i = h[q], h[slabs_per_step + q]
                    b_r, b_i = bur[s0 + q, rows_t, :], bui[s0 + q, rows_t, :]
                    n_r = ar[q] * h_r - ai[q] * h_i + b_r
                    n_i = ar[q] * h_i + ai[q] * h_r + b_i
                    if spr == 2:
                        p_r, p_i = pltpu.roll(n_r, spv, axis=0), pltpu.roll(n_i, spv, axis=0)
                        m_r = ar[q] * p_r - ai[q] * p_i + b_r
                        m_i = ar[q] * p_i + ai[q] * p_r + b_i
                        n_r, n_i = jnp.where(low, n_r, m_r), jnp.where(low, n_i, m_i)
                        h_r = jnp.where(low, pltpu.roll(m_r, spv, axis=0), m_r)
                        h_i = jnp.where(low, pltpu.roll(m_i, spv, axis=0), m_i)
                    else:
                        h_r, h_i = n_r, n_i
                    bur[s0 + q, rows_t, :] = n_r
                    bui[s0 + q, rows_t, :] = n_i
                    out_r.append(h_r)
                    out_i.append(h_i)
                return tuple(out_r) + tuple(out_i)

            h = lax.fori_loop(0, tt // spr, step, h0)
            for q, c in enumerate(sl):
                str_[g * spv:(g + 1) * spv, c] = h[q][0:spv]
                sti[g * spv:(g + 1) * spv, c] = h[slabs_per_step + q][0:spv]

    hr_ref[...] = str_[...]
    hi_ref[...] = sti[...]

    for j in range(n_chunk):
        hcat = jnp.concatenate([bur[j * spc + q] for q in range(spc)] + [bui[j * spc + q] for q in range(spc)], axis=1)
        yj = _dot(hcat.astype(BF16), wcbf[j]) + d_ref[:, j * ck:(j + 1) * ck] * utm[j]
        utm[j] = _gelu_tanh(yj)
    for s in range(nb):
        for j in range(n_chunk):
            y_ref[s, :, j * ck:(j + 1) * ck] = utm[j, pl.ds(s, tt, stride=nb), :]


def s5_scan(u3, h0r, h0i, wb, ar, ai, wc, d_skip, *, nb, tt, passes, name):
    B, T, D = u3.shape
    S = h0r.shape[1]
    n_chunk = wb.shape[0]
    kern = functools.partial(_s5_kernel, nb=nb, tt=tt, n_chunk=n_chunk, slabs_per_step=8, passes=passes)
    full = lambda a: pl.BlockSpec(a.shape, lambda b, t: (0,) * a.ndim)
    return pl.pallas_call(
        kern,
        grid=(B // nb, T // tt),
        in_specs=[pl.BlockSpec((nb, tt, D), lambda b, t: (b, t, 0)),
                  pl.BlockSpec((nb, S), lambda b, t: (b, 0)),
                  pl.BlockSpec((nb, S), lambda b, t: (b, 0)),
                  full(wb), full(ar), full(ai), full(wc), full(d_skip)],
        out_specs=[pl.BlockSpec((nb, tt, D), lambda b, t: (b, t, 0)),
                   pl.BlockSpec((nb, S), lambda b, t: (b, 0)),
                   pl.BlockSpec((nb, S), lambda b, t: (b, 0))],
        out_shape=[jax.ShapeDtypeStruct((B, T, D), F32),
                   jax.ShapeDtypeStruct((B, S), F32),
                   jax.ShapeDtypeStruct((B, S), F32)],
        scratch_shapes=[pltpu.VMEM((D // LANES, nb * tt, LANES), F32),
                        pltpu.VMEM((S // LANES, nb * tt, LANES), F32), pltpu.VMEM((S // LANES, nb * tt, LANES), F32),
                        pltpu.VMEM((nb, S), F32), pltpu.VMEM((nb, S), F32),
                        pltpu.VMEM(wb.shape, BF16), pltpu.VMEM(wb.shape, BF16), pltpu.VMEM(wc.shape, BF16)],
        compiler_params=_params(("arbitrary", "arbitrary")),
        name=name,
    )(u3, h0r, h0i, wb, ar, ai, wc, d_skip)


def s5_tables(log_dt, a_re, a_im, b_re, b_im, c_re, c_im, n_chunk):
    ng, p = a_re.shape
    ch = b_re.shape[-1]
    dt = jnp.exp(log_dt)[:, None]
    mag, ang = jnp.exp(dt * a_re), dt * a_im
    abr, abi = mag * jnp.cos(ang), mag * jnp.sin(ang)
    den = a_re * a_re + a_im * a_im
    fr = ((abr - 1.0) * a_re + abi * a_im) / den
    fi = (abi * a_re - (abr - 1.0) * a_im) / den
    bbr = fr[..., None] * b_re - fi[..., None] * b_im
    bbi = fr[..., None] * b_im + fi[..., None] * b_re
    gpc = ng // n_chunk
    eye = jnp.eye(gpc, dtype=F32)

    def in_proj(bb):
        b4 = bb.reshape(n_chunk, gpc, p, ch)
        return jnp.einsum("jgpc,gh->jgchp", b4, eye).reshape(n_chunk, gpc * ch, gpc * p)

    def out_proj(cc):
        c4 = cc.reshape(n_chunk, gpc, ch, p)
        return jnp.einsum("jgcp,gh->jgphc", c4, eye).reshape(n_chunk, gpc * p, gpc * ch)

    wb = jnp.concatenate([in_proj(bbr), in_proj(bbi)], axis=2)
    wc = jnp.concatenate([out_proj(c_re), -out_proj(c_im)], axis=1)
    return wb, abr.reshape(1, ng * p), abi.reshape(1, ng * p), wc


def _router_kernel(x_ref, g_ref, sh_ref, sc_ref, w_ref, b_ref, h_ref, gate_ref, idx_ref, rank_ref, cnt_ref,
                   whi, wlo, count, *, bb, rows):
    @pl.when(pl.program_id(0) == 0)
    def _():
        w = w_ref[...]
        hi = w.astype(BF16)
        whi[...] = hi
        wlo[...] = (w - hi.astype(F32)).astype(BF16)
        count[...] = jnp.zeros(count.shape, F32)

    x = x_ref[...]
    ms = jnp.mean(x * x, axis=-1, keepdims=True)
    x = x * lax.rsqrt(ms + EPS) * g_ref[...]
    x = x * (1.0 + sc_ref[...]) + sh_ref[...]
    h_ref[...] = x
    tm = bb * rows
    x2 = x.reshape(tm, x.shape[-1])
    xh, xl = _split2(x2)
    logits = _dot(xh, whi[...]) + _dot(xl, whi[...]) + _dot(xh, wlo[...]) + b_ref[...]
    ne = logits.shape[1]
    lane = lax.broadcasted_iota(jnp.int32, logits.shape, 1).astype(F32)
    kcol = lax.broadcasted_iota(jnp.int32, (tm, TOP_K), 1)
    vals = jnp.zeros((tm, TOP_K), F32)
    idxs = jnp.zeros((tm, TOP_K), F32)
    cur = logits
    hot = []
    for k in range(TOP_K):
        mx = jnp.max(cur, axis=1, keepdims=True)
        am = jnp.min(jnp.where(cur == mx, lane, float(ne)), axis=1, keepdims=True)
        vals = jnp.where(kcol == k, mx, vals)
        idxs = jnp.where(kcol == k, am, idxs)
        sel = lane == am
        hot.append(jnp.where(sel, 1.0, 0.0))
        cur = jnp.where(sel, -jnp.inf, cur)
    e = jnp.exp(vals - vals[:, 0:1])
    gate_ref[...] = e / jnp.sum(e, axis=1, keepdims=True)
    idx_ref[...] = idxs.astype(jnp.int32)
    cnt = hot[0] + hot[1] + hot[2] + hot[3]
    r = lax.broadcasted_iota(jnp.int32, (tm, tm), 0)
    c = lax.broadcasted_iota(jnp.int32, (tm, tm), 1)
    before = jnp.where(c < r, 1.0, 0.0).astype(BF16)
    base = _dot(before, cnt.astype(BF16)) + count[...]
    ranks = jnp.zeros((tm, TOP_K), F32)
    for k in range(TOP_K):
        ranks = jnp.where(kcol == k, jnp.sum(hot[k] * base, axis=1, keepdims=True), ranks)
    rank_ref[...] = ranks.astype(jnp.int32)
    count[...] += jnp.sum(cnt, axis=0, keepdims=True)
    cnt_ref[...] = count[...]


def moe_router(x3, gain, layer, modexp, w_router, b_router):
    G, R, K = x3.shape
    E = w_router.shape[-1]
    bb = _pick(G, (64, 32, 16, 8, 4, 2, 1))
    tm = bb * R
    T = G * R
    kern = functools.partial(_router_kernel, bb=bb, rows=R)
    return pl.pallas_call(
        kern,
        grid=(G // bb,),
        in_specs=[pl.BlockSpec((bb, R, K), lambda g: (g, 0, 0)),
                  pl.BlockSpec((None, 1, K), lambda g: (layer, 0, 0)),
                  pl.BlockSpec((None, bb, 1, K), lambda g: (3, g, 0, 0)),
                  pl.BlockSpec((None, bb, 1, K), lambda g: (4, g, 0, 0)),
                  pl.BlockSpec((None, K, E), lambda g: (layer, 0, 0)),
                  pl.BlockSpec((None, 1, E), lambda g: (layer, 0, 0))],
        out_specs=[pl.BlockSpec((bb, R, K), lambda g: (g, 0, 0)),
                   pl.BlockSpec((tm, TOP_K), lambda g: (g, 0)),
                   pl.BlockSpec((tm, TOP_K), lambda g: (g, 0)),
                   pl.BlockSpec((tm, TOP_K), lambda g: (g, 0)),
                   pl.BlockSpec((1, E), lambda g: (0, 0))],
        out_shape=[jax.ShapeDtypeStruct((G, R, K), F32),
                   jax.ShapeDtypeStruct((T, TOP_K), F32),
                   jax.ShapeDtypeStruct((T, TOP_K), jnp.int32),
                   jax.ShapeDtypeStruct((T, TOP_K), jnp.int32),
                   jax.ShapeDtypeStruct((1, E), F32)],
        scratch_shapes=[pltpu.VMEM((K, E), BF16), pltpu.VMEM((K, E), BF16), pltpu.VMEM((1, E), F32)],
        compiler_params=_params(("arbitrary",)),
        name="moe_router",
    )(x3, gain.reshape(gain.shape[0], 1, K), modexp, modexp, w_router,
      b_router.reshape(b_router.shape[0], 1, E))


def _dispatch_kernel(dest_ref, h_ref, xs_in, xs_hbm, sem, *, tm):
    del xs_in
    i = pl.program_id(0)

    def issue(r8, c):
        row0 = pl.multiple_of(r8 * GROUP, GROUP)
        base = (i * tm + row0) * TOP_K
        for j in range(GROUP):
            for k in range(TOP_K):
                pltpu.make_async_copy(h_ref.at[pl.ds(row0 + j, 1), :],
                                      xs_hbm.at[pl.ds(dest_ref[base + j * TOP_K + k], 1), :], sem.at[0]).start()
        return c

    lax.fori_loop(0, tm // GROUP, issue, 0)
    for k in range(TOP_K):
        pltpu.make_async_copy(h_ref, xs_hbm.at[pl.ds(0, tm), :], sem.at[0]).wait()


def moe_dispatch(h2, dest, n_slots):
    T, D = h2.shape
    tm = _pick(T, (256, 128, 64, 32, 16, 8))
    xs0 = jnp.zeros((n_slots, D), F32)
    grid_spec = pltpu.PrefetchScalarGridSpec(
        num_scalar_prefetch=1,
        grid=(T // tm,),
        in_specs=[pl.BlockSpec((tm, D), lambda i, d: (i, 0)), pl.BlockSpec(memory_space=pl.ANY)],
        out_specs=pl.BlockSpec(memory_space=pl.ANY),
        scratch_shapes=[pltpu.SemaphoreType.DMA((1,))],
    )
    return pl.pallas_call(
        functools.partial(_dispatch_kernel, tm=tm),
        grid_spec=grid_spec,
        out_shape=jax.ShapeDtypeStruct((n_slots, D), F32),
        input_output_aliases={2: 0},
        compiler_params=_params(("arbitrary",)),
        name="moe_dispatch",
    )(dest, h2, xs0)


def _expert_kernel(bexp_ref, nblk_ref, x_ref, wgu_ref, bgu_ref, wd_ref, bd_ref, o_ref, wgu_bf, wd_bf, *, d_ff):
    i = pl.program_id(0)
    n_used = nblk_ref[0]

    @pl.when(i < n_used)
    def _():
        changed = jnp.logical_or(i == 0, bexp_ref[i] != bexp_ref[jnp.maximum(i - 1, 0)])

        @pl.when(changed)
        def _():
            wgu_bf[...] = wgu_ref[...].astype(BF16)
            wd_bf[...] = wd_ref[...].astype(BF16)

        gu = _dot(x_ref[...].astype(BF16), wgu_bf[...]) + bgu_ref[...]
        gate = jnp.minimum(gu[:, :d_ff], SWIGLU_LIMIT)
        up = jnp.clip(gu[:, d_ff:], -SWIGLU_LIMIT, SWIGLU_LIMIT)
        act = (up + 1.0) * gate * _sigmoid(SWIGLU_ALPHA * gate)
        o_ref[...] = _dot(act.astype(BF16), wd_bf[...]) + bd_ref[...]

    @pl.when(i >= n_used)
    def _():
        o_ref[...] = jnp.zeros(o_ref.shape, F32)


def moe_experts(xs, block_expert, n_used, w_gu, b_gu, w_down, b_down, *, layer, blk):
    n_slots, D = xs.shape
    n_blocks = n_slots // blk
    E, _, F2 = w_gu.shape[1:]
    d_ff = F2 // 2
    grid_spec = pltpu.PrefetchScalarGridSpec(
        num_scalar_prefetch=2,
        grid=(n_blocks,),
        in_specs=[pl.BlockSpec((blk, D), lambda i, be, nu: (jnp.minimum(i, nu[0] - 1), 0)),
                  pl.BlockSpec((None, None, D, F2), lambda i, be, nu: (layer, be[i], 0, 0)),
                  pl.BlockSpec((None, None, 1, F2), lambda i, be, nu: (layer, be[i], 0, 0)),
                  pl.BlockSpec((None, None, d_ff, D), lambda i, be, nu: (layer, be[i], 0, 0)),
                  pl.BlockSpec((None, None, 1, D), lambda i, be, nu: (layer, be[i], 0, 0))],
        out_specs=pl.BlockSpec((blk, D), lambda i, be, nu: (i, 0)),
        scratch_shapes=[pltpu.VMEM((D, F2), BF16), pltpu.VMEM((d_ff, D), BF16)],
    )
    return pl.pallas_call(
        functools.partial(_expert_kernel, d_ff=d_ff),
        grid_spec=grid_spec,
        out_shape=jax.ShapeDtypeStruct((n_slots, D), F32),
        compiler_params=_params(("arbitrary",)),
        name="moe_experts",
    )(block_expert, n_used, xs, w_gu, b_gu.reshape(b_gu.shape[0], E, 1, F2), w_down,
      b_down.reshape(b_down.shape[0], E, 1, D))


def _combine_kernel(dest_ref, yb_hbm, gates_ref, res_ref, gate_ref, o_ref, buf, sem, *, tm, bb, rows):
    i = pl.program_id(0)
    n = pl.num_programs(0)
    slot = i % 2

    def gather(tile, s):
        def issue(r8, c):
            row0 = pl.multiple_of(r8 * GROUP, GROUP)
            base = (tile * tm + row0) * TOP_K
            for j in range(GROUP):
                for k in range(TOP_K):
                    src = dest_ref[base + j * TOP_K + k]
                    pltpu.make_async_copy(yb_hbm.at[pl.ds(src, 1), :], buf.at[s, k, pl.ds(row0 + j, 1), :],
                                          sem.at[s]).start()
            return c
        lax.fori_loop(0, tm // GROUP, issue, 0)

    @pl.when(i == 0)
    def _():
        gather(0, 0)

    @pl.when(i + 1 < n)
    def _():
        gather(i + 1, 1 - slot)

    for k in range(TOP_K):
        pltpu.make_async_copy(yb_hbm.at[pl.ds(0, tm), :], buf.at[slot, k], sem.at[slot]).wait()
    g = gates_ref[...]
    y = g[:, 0:1] * buf[slot, 0]
    for k in range(1, TOP_K):
        y = y + g[:, k:k + 1] * buf[slot, k]
    o_ref[...] = res_ref[...] + gate_ref[...] * y.reshape(bb, rows, y.shape[-1])


def moe_combine(yb, dest, gates, xres3, modexp):
    G, R, D = xres3.shape
    bb = _pick(G, (16, 8, 4, 2, 1))
    tm = bb * R
    grid_spec = pltpu.PrefetchScalarGridSpec(
        num_scalar_prefetch=1,
        grid=(G // bb,),
        in_specs=[pl.BlockSpec(memory_space=pl.ANY),
                  pl.BlockSpec((tm, TOP_K), lambda g, d: (g, 0)),
                  pl.BlockSpec((bb, R, D), lambda g, d: (g, 0, 0)),
                  pl.BlockSpec((None, bb, 1, D), lambda g, d: (5, g, 0, 0))],
        out_specs=pl.BlockSpec((bb, R, D), lambda g, d: (g, 0, 0)),
        scratch_shapes=[pltpu.VMEM((2, TOP_K, tm, D), F32), pltpu.SemaphoreType.DMA((2,))],
    )
    return pl.pallas_call(
        functools.partial(_combine_kernel, tm=tm, bb=bb, rows=R),
        grid_spec=grid_spec,
        out_shape=jax.ShapeDtypeStruct((G, R, D), F32),
        compiler_params=_params(("arbitrary",)),
        name="moe_combine",
    )(dest, yb, gates, xres3, modexp)


def moe_layer(x3, modexp, layer, norm_ffn, w_router, b_router, w_gu, b_gu, w_down, b_down, blk):
    G, R, D = x3.shape
    T = G * R
    E = w_router.shape[-1]
    h3, gates, idx, rank, counts = moe_router(x3, norm_ffn, layer, modexp, w_router, b_router)
    counts = counts.reshape(E).astype(jnp.int32)
    padded = (counts + blk - 1) // blk * blk
    pad_end = jnp.cumsum(padded)
    pad_start = pad_end - padded
    experts = jnp.arange(E, dtype=jnp.int32)
    dest = rank + jnp.sum(jnp.where(idx[..., None] == experts, pad_start, 0), axis=-1)
    dest = dest.reshape(-1).astype(jnp.int32)
    n_blocks = -(-(T * TOP_K) // blk) + E
    block_start = jnp.arange(n_blocks, dtype=jnp.int32) * blk
    block_expert = jnp.minimum(jnp.sum(pad_end[None, :] <= block_start[:, None], axis=1), E - 1).astype(jnp.int32)
    n_used = (pad_end[-1:] // blk).astype(jnp.int32)
    xs = moe_dispatch(h3.reshape(T, D), dest, n_blocks * blk)
    yb = moe_experts(xs, block_expert, n_used, w_gu, b_gu, w_down, b_down, layer=layer, blk=blk)
    return moe_combine(yb, dest, gates, x3, modexp)


def kernel(x_prompt, x_sample, cache_a_k, cache_a_v, state_b_re, state_b_im, cache_c_k, cache_c_v, page_table, c_prompt, c_sample, w_ada, b_ada, norm_mix, norm_ffn, a_w_in, a_q_norm, a_k_norm, a_lambda, a_sub_norm, a_w_out, b_log_dt, b_a_re, b_a_im, b_b_re, b_b_im, b_c_re, b_c_im, b_d, b_w_glu, b_b_glu, c_w_in, c_w_out, moe_w_router, moe_b_router, moe_w_gu, moe_b_gu, moe_w_down, moe_b_down):
    batch, seq, d = x_prompt.shape
    nseq, dec_seq, _ = x_sample.shape
    assert dec_seq == GROUP and seq % GROUP == 0
    depth = w_ada.shape[0]
    n_pages, page = page_table.shape[1], cache_a_k.shape[2]
    past = n_pages * page
    tp, ts = batch * seq, nseq * dec_seq
    T = tp + ts
    G = T // GROUP
    gp = tp // GROUP
    h_a, d_ha = cache_a_k.shape[3], cache_a_k.shape[5]
    h_c, d_hc = cache_c_k.shape[3], cache_c_k.shape[4]
    ssm_g, ssm_p = b_a_re.shape[1], b_a_re.shape[2]
    n_state = ssm_g * ssm_p

    x3 = jnp.concatenate([x_prompt.reshape(gp, GROUP, d), x_sample], axis=0)

    nc = batch + nseq
    ncp = -(-nc // GROUP) * GROUP
    c_all = jnp.concatenate([c_prompt, c_sample, jnp.zeros((ncp - nc, d), F32)], axis=0).reshape(1, ncp, d)
    modexps = []
    for i in range(depth):
        m = fused_matmul(c_all, w_ada, b_ada, layer=i, silu_in=True, passes=3, name="ada")[0]
        m = m.reshape(ncp, 6, d).transpose(1, 0, 2)
        mexp = jnp.concatenate([jnp.repeat(m[:, :batch], seq // GROUP, axis=1), m[:, batch:nc]], axis=1)
        modexps.append(mexp.reshape(6, G, 1, d))

    pos = jnp.concatenate([jnp.tile(jnp.arange(seq, dtype=jnp.int32), batch),
                           jnp.tile(past + jnp.arange(dec_seq, dtype=jnp.int32), nseq)])
    freqs = ROPE_THETA ** (-jnp.arange(0, d_ha, 2, dtype=F32) / d_ha)
    ang = pos.astype(F32)[:, None] * freqs[None, :]
    cos, sin = jnp.cos(ang), jnp.sin(ang)
    reps = LANES // d_ha
    cos_t = jnp.tile(jnp.concatenate([cos, cos], axis=1), (1, reps))
    sin_t = jnp.tile(jnp.concatenate([-sin, sin], axis=1), (1, reps))

    zeros_state = jnp.zeros((batch, n_state), F32)
    moe_blk = 256 if T * TOP_K >= 32768 else 128
    new_a, new_b, new_c = [], [], []
    for i in range(depth):
        kind, j = i % N_MIXERS, i // N_MIXERS
        mexp = modexps[i]
        if kind == 0:
            lam_init = 0.8 - 0.6 * math.exp(-0.3 * i)
            nqk = h_a * 2 * d_ha
            qkv3 = fused_matmul(x3, a_w_in, layer=j, norm=(norm_mix, i, mexp, 0, 1), name="a_qkv")
            qkv2 = qkv3.reshape(T, 3 * nqk)
            gains = jnp.stack([jnp.tile(a_q_norm[j], reps) * (d_ha ** -0.5), jnp.tile(a_k_norm[j], reps)])
            qk2 = qknorm_rope(qkv2, gains.reshape(2, 1, LANES), cos_t, sin_t, 2 * nqk, d_ha)
            o2 = diff_attention_prompt(qk2, qkv2, a_sub_norm, a_lambda, batch=batch, seq=seq, n_heads=h_a,
                                       d_head=d_ha, lam_init=lam_init, layer=j)
            o2 = diff_attention_sample(o2, qk2, qkv2, cache_a_k, cache_a_v, page_table, a_sub_norm, a_lambda,
                                       n_prompt_rows=tp, n_heads=h_a, d_head=d_ha, nq=dec_seq,
                                       lam_init=lam_init, layer=j)
            x3 = fused_matmul(o2.reshape(G, GROUP, nqk), a_w_out, layer=j, resid=(x3, mexp, 2), name="a_out")
            k2, v2 = qk2[:, nqk:], qkv2[:, 2 * nqk:]
            new_a.append((k2[:tp].reshape(batch, seq, h_a, 2, d_ha), v2[:tp].reshape(batch, seq, h_a, 2 * d_ha),
                          k2[tp:].reshape(nseq, dec_seq, h_a, 2, d_ha), v2[tp:].reshape(nseq, dec_seq, h_a, 2 * d_ha)))
        elif kind == 1:
            u3 = mod_norm(x3, norm_mix, i, mexp, 0, 1)
            wb, ar, ai, wc = s5_tables(b_log_dt[j], b_a_re[j], b_a_im[j], b_b_re[j], b_b_im[j],
                                       b_c_re[j], b_c_im[j], ssm_g // GROUP)
            d_skip = b_d[j].reshape(1, d)
            y_p, hpr, hpi = s5_scan(u3[:gp].reshape(batch, seq, d), zeros_state, zeros_state, wb, ar, ai, wc,
                                    d_skip, nb=batch, tt=_pick(seq, (64, 32, 16, 8)), passes=1, name="s5_prompt")
            y_s, hsr, hsi = s5_scan(u3[gp:], state_b_re[j].reshape(nseq, n_state), state_b_im[j].reshape(nseq, n_state),
                                    wb, ar, ai, wc, d_skip, nb=_pick(nseq, (32, 16, 8)), tt=dec_seq, passes=3,
                                    name="s5_sample")
            y3 = jnp.concatenate([y_p.reshape(gp, GROUP, d), y_s], axis=0)
            x3 = fused_matmul(y3, b_w_glu, b_b_glu, layer=j, glu=True, resid=(x3, mexp, 2), name="b_glu")
            st = lambda a, n: a.reshape(n, ssm_g, ssm_p)
            new_b.append((st(hpr, batch), st(hpi, batch), st(hsr, nseq), st(hsi, nseq)))
        else:
            dm = h_c * d_hc
            qkv3 = fused_matmul(x3, c_w_in, layer=j, norm=(norm_mix, i, mexp, 0, 1), name="c_qkv")
            qkv2 = qkv3.reshape(T, 3 * dm)
            o2 = sb_attention_prompt(qkv2, batch=batch, seq=seq, n_heads=h_c, d_head=d_hc)
            o2 = sb_attention_sample(o2, qkv2, cache_c_k, cache_c_v, page_table, n_prompt_rows=tp,
                                     n_heads=h_c, d_head=d_hc, nq=dec_seq, layer=j)
            x3 = fused_matmul(o2.reshape(G, GROUP, dm), c_w_out, layer=j, resid=(x3, mexp, 2), name="c_out")
            k2, v2 = qkv2[:, dm:2 * dm], qkv2[:, 2 * dm:]
            new_c.append((k2[:tp].reshape(batch, seq, h_c, d_hc), v2[:tp].reshape(batch, seq, h_c, d_hc),
                          k2[tp:].reshape(nseq, dec_seq, h_c, d_hc), v2[tp:].reshape(nseq, dec_seq, h_c, d_hc)))
        x3 = moe_layer(x3, mexp, i, norm_ffn, moe_w_router, moe_b_router, moe_w_gu, moe_b_gu,
                       moe_w_down, moe_b_down, moe_blk)

    outs = [x3[:gp].reshape(batch, seq, d), x3[gp:]]
    for group in (new_a, new_b, new_c):
        for k in range(4):
            outs.append(jnp.stack([entry[k] for entry in group]))
    return tuple(outs)
```
